```python
import math
import jax, jax.numpy as jnp
from jax import lax
import numpy as np

D_MODEL = 1024
BATCH = 16
SEQ = 2048
DEPTH = 1
DEC_BATCH = 32
DEC_SEQ = 64
PAST_LEN = 4096

CHUNK = 64
Q_BLOCK = 128
DA_HEADS = 4
DA_HD = 64
DA_VD = 2 * DA_HD
DA_QK_W = DA_HEADS * 2 * DA_HD
DA_V_W = DA_HEADS * DA_VD
RET_HEADS = 4
RET_QK = 64
RET_VD = 128
RET_QK_W = RET_HEADS * RET_QK
RET_V_W = RET_HEADS * RET_VD
ROPE_BASE = 10000.0
D_FF = 2816
N_MOD = 9
IN_SIZES = (DA_QK_W, DA_QK_W, DA_V_W, RET_QK_W, RET_QK_W, RET_V_W, RET_V_W, D_MODEL, D_MODEL)
IN_WIDTH = 2 * DA_QK_W + DA_V_W + 2 * RET_QK_W + 2 * RET_V_W + 2 * D_MODEL
NEG_INF = -1e30
EPS = 1e-6

kernel_name = 'diffattn_retention_macaron_stream'


def rms_norm(x, w):
    xf = x.astype(jnp.float32)
    y = xf * lax.rsqrt(jnp.mean(xf * xf, axis=-1, keepdims=True) + EPS)
    return (y * w.astype(jnp.float32)).astype(x.dtype)


def head_rms_norm(x, w):
    xf = x.astype(jnp.float32)
    y = xf * lax.rsqrt(jnp.mean(xf * xf, axis=-1, keepdims=True) + EPS)
    return (y * w.astype(jnp.float32)).astype(x.dtype)


def head_group_norm(x, w):
    xf = x.astype(jnp.float32)
    mu = jnp.mean(xf, axis=-1, keepdims=True)
    xc = xf - mu
    y = xc * lax.rsqrt(jnp.mean(xc * xc, axis=-1, keepdims=True) + EPS)
    return (y * w.astype(jnp.float32)).astype(x.dtype)


def modulate(h, shift, scale):
    return h * (1.0 + scale) + shift


def swiglu(h, w_up, w_down):
    a, b = jnp.split(h @ w_up, 2, axis=-1)
    return (jax.nn.silu(a) * b) @ w_down


def rotary(x, pos):
    half = x.shape[-1] // 2
    inv = ROPE_BASE ** (-jnp.arange(half, dtype=jnp.float32) / half)
    ang = pos.astype(jnp.float32)[:, None] * inv[None, :]
    cos = jnp.cos(ang)[None, :, None, :]
    sin = jnp.sin(ang)[None, :, None, :]
    xf = x.astype(jnp.float32)
    x1, x2 = xf[..., :half], xf[..., half:]
    return jnp.concatenate([x1 * cos - x2 * sin, x1 * sin + x2 * cos], axis=-1)


def diff_attention(q, k, v, q_pos, k_pos, lam):
    s = jnp.einsum('bqhcd,bkhcd->bhcqk', q, k).astype(jnp.float32)
    visible = (k_pos[None, :] // CHUNK) <= (q_pos[:, None] // CHUNK)
    p = jax.nn.softmax(jnp.where(visible, s, NEG_INF), axis=-1)
    a = p[:, :, 0] - lam * p[:, :, 1]
    return jnp.einsum('bhqk,bkhv->bqhv', a.astype(v.dtype), v)


def diff_attention_prompt(q, k, v, lam):
    B, S = q.shape[0], q.shape[1]
    nb = S // Q_BLOCK
    pos = jnp.arange(S)
    qb = jnp.moveaxis(q.reshape(B, nb, Q_BLOCK, DA_HEADS, 2, DA_HD), 1, 0)
    pb = pos.reshape(nb, Q_BLOCK)
    out = lax.map(lambda a: diff_attention(a[0], k, v, a[1], pos, lam), (qb, pb))
    return jnp.moveaxis(out, 0, 1).reshape(B, S, DA_HEADS, DA_VD)


def retention_block(q, k, v, state, log_g):
    L = q.shape[1]
    idx = jnp.arange(L, dtype=jnp.float32)
    dist = idx[:, None] - idx[None, :]
    decay = jnp.where(dist >= 0, jnp.exp(jnp.maximum(dist, 0.0)[None] * log_g[:, None, None]), 0.0)
    scores = jnp.einsum('blhd,bmhd->bhlm', q, k) * decay[None]
    o_inner = jnp.einsum('bhlm,bmhv->blhv', scores, v)
    q_decay = jnp.exp((idx + 1.0)[:, None] * log_g[None, :])
    o_cross = jnp.einsum('blhd,bhdv->blhv', q, state) * q_decay[None, :, :, None]
    k_decay = jnp.exp((L - 1.0 - idx)[:, None] * log_g[None, :])
    new_state = (jnp.exp(L * log_g)[None, :, None, None] * state
                 + jnp.einsum('blhd,blhv->bhdv', k * k_decay[None, :, :, None], v))
    return o_inner + o_cross, new_state


def retention_prompt(q, k, v, log_g):
    B, S = q.shape[0], q.shape[1]
    nc = S // CHUNK

    def to_chunks(t):
        return jnp.moveaxis(t.reshape(B, nc, CHUNK, *t.shape[2:]), 1, 0)

    s0 = jnp.zeros((B, RET_HEADS, RET_QK, RET_VD), jnp.float32)

    def step(state, xs):
        o, new_state = retention_block(xs[0], xs[1], xs[2], state, log_g)
        return new_state, o

    final, out = lax.scan(step, s0, (to_chunks(q), to_chunks(k), to_chunks(v)))
    return jnp.moveaxis(out, 0, 1).reshape(B, S, RET_HEADS, RET_VD), final


def layer_step(x, c, past_k, past_v, past_s, lam_init,
               norm_f1, w_up1, w_down1, norm_mix, w_in,
               lambda_q1, lambda_k1, lambda_q2, lambda_k2, da_norm, ret_norm,
               w_a_proj, w_r_proj, w_o, norm_f2, w_up2, w_down2, w_ada, b_ada):
    B, L, _ = x.shape
    mod = (jax.nn.silu(c) @ w_ada + b_ada).reshape(B, N_MOD, D_MODEL)[:, :, None, :]
    sh1, sc1, g1, shm, scm, gm, sh2, sc2, g2 = [mod[:, i] for i in range(N_MOD)]

    x = x + 0.5 * g1 * swiglu(modulate(rms_norm(x, norm_f1), sh1, sc1), w_up1, w_down1)

    hm = modulate(rms_norm(x, norm_mix), shm, scm)
    offsets = np.cumsum(IN_SIZES)[:-1].tolist()
    q_a, k_a, v_a, q_r, k_r, v_r, z_r, gate_a, gate_r = jnp.split(hm @ w_in, offsets, axis=-1)
    q_a = q_a.reshape(B, L, DA_HEADS, 2, DA_HD) * (DA_HD ** -0.5)
    k_a = k_a.reshape(B, L, DA_HEADS, 2, DA_HD)
    v_a = v_a.reshape(B, L, DA_HEADS, DA_VD)

    offset = 0 if past_k is None else past_k.shape[1]
    pos = offset + jnp.arange(L)
    q_r = rotary(q_r.reshape(B, L, RET_HEADS, RET_QK), pos)
    k_r = rotary(k_r.reshape(B, L, RET_HEADS, RET_QK), pos) * (RET_QK ** -0.5)
    v_r = v_r.reshape(B, L, RET_HEADS, RET_VD).astype(jnp.float32)
    log_g = jnp.log(1.0 - 2.0 ** (-5.0 - jnp.arange(RET_HEADS, dtype=jnp.float32)))

    f32 = jnp.float32
    lam = (jnp.exp(jnp.sum(lambda_q1.astype(f32) * lambda_k1.astype(f32)))
           - jnp.exp(jnp.sum(lambda_q2.astype(f32) * lambda_k2.astype(f32))) + lam_init)

    if past_k is None:
        o_a = diff_attention_prompt(q_a, k_a, v_a, lam)
        o_r, s_new = retention_prompt(q_r, k_r, v_r, log_g)
        s_new = s_new.astype(x.dtype)
    else:
        k_all = jnp.concatenate([past_k.astype(k_a.dtype), k_a], axis=1)
        v_all = jnp.concatenate([past_v.astype(v_a.dtype), v_a], axis=1)
        o_a = diff_attention(q_a, k_all, v_all, pos, jnp.arange(offset + L), lam)
        o_r, s_new = retention_block(q_r, k_r, v_r, past_s.astype(f32), log_g)
        s_new = s_new.astype(past_s.dtype)

    o_a = (head_rms_norm(o_a, da_norm) * (1.0 - lam_init)).reshape(B, L, DA_V_W)
    o_r = head_group_norm(o_r, ret_norm).astype(x.dtype).reshape(B, L, RET_V_W) * jax.nn.silu(z_r)
    merged = jax.nn.sigmoid(gate_a) * (o_a @ w_a_proj) + jax.nn.sigmoid(gate_r) * (o_r @ w_r_proj)
    x = x + gm * (merged @ w_o)

    x = x + 0.5 * g2 * swiglu(modulate(rms_norm(x, norm_f2), sh2, sc2), w_up2, w_down2)
    return x, k_a, v_a, s_new


def setup_inputs(seed: int = 0) -> dict:
    key = jax.random.key(seed)
    ks = jax.random.split(key, 32)
    f32 = jnp.float32
    D = D_MODEL

    def nrm(k, shape, scale):
        return jax.random.normal(k, shape, f32) * scale

    def gain(k, shape):
        return 1.0 + 0.02 * jax.random.normal(k, shape, f32)

    return {
        'x_prompt': nrm(ks[0], (BATCH, SEQ, D), 1.0),
        'x_sample': nrm(ks[1], (DEC_BATCH, DEC_SEQ, D), 1.0),
        'c_prompt': nrm(ks[2], (BATCH, D), 1.0),
        'c_sample': nrm(ks[3], (DEC_BATCH, D), 1.0),
        'cache_k': nrm(ks[4], (DEPTH, DEC_BATCH, PAST_LEN, DA_HEADS, 2, DA_HD), 1.0),
        'cache_v': nrm(ks[5], (DEPTH, DEC_BATCH, PAST_LEN, DA_HEADS, DA_VD), 1.0),
        'state_ret': nrm(ks[6], (DEPTH, DEC_BATCH, RET_HEADS, RET_QK, RET_VD), 0.5),
        'norm_f1': gain(ks[7], (DEPTH, D)),
        'w_up1': nrm(ks[8], (DEPTH, D, 2 * D_FF), D ** -0.5),
        'w_down1': nrm(ks[9], (DEPTH, D_FF, D), D_FF ** -0.5),
        'norm_mix': gain(ks[10], (DEPTH, D)),
        'w_in': nrm(ks[11], (DEPTH, D, IN_WIDTH), D ** -0.5),
        'lambda_q1': nrm(ks[12], (DEPTH, DA_HD), 0.1),
        'lambda_k1': nrm(ks[13], (DEPTH, DA_HD), 0.1),
        'lambda_q2': nrm(ks[14], (DEPTH, DA_HD), 0.1),
        'lambda_k2': nrm(ks[15], (DEPTH, DA_HD), 0.1),
        'da_norm': gain(ks[16], (DEPTH, DA_HEADS, DA_VD)),
        'ret_norm': gain(ks[17], (DEPTH, RET_HEADS, RET_VD)),
        'w_a_proj': nrm(ks[18], (DEPTH, DA_V_W, D), DA_V_W ** -0.5),
        'w_r_proj': nrm(ks[19], (DEPTH, RET_V_W, D), RET_V_W ** -0.5),
        'w_o': nrm(ks[20], (DEPTH, D, D), D ** -0.5),
        'norm_f2': gain(ks[21], (DEPTH, D)),
        'w_up2': nrm(ks[22], (DEPTH, D, 2 * D_FF), D ** -0.5),
        'w_down2': nrm(ks[23], (DEPTH, D_FF, D), D_FF ** -0.5),
        'w_ada': nrm(ks[24], (DEPTH, D, N_MOD * D), D ** -0.5),
        'b_ada': nrm(ks[25], (DEPTH, N_MOD * D), 0.01),
        'norm_final': gain(ks[26], (D,)),
    }


def reference(x_prompt, x_sample, c_prompt, c_sample, cache_k, cache_v, state_ret,
              norm_f1, w_up1, w_down1, norm_mix, w_in,
              lambda_q1, lambda_k1, lambda_q2, lambda_k2, da_norm, ret_norm,
              w_a_proj, w_r_proj, w_o, norm_f2, w_up2, w_down2, w_ada, b_ada, norm_final):
    hp, hs = x_prompt, x_sample
    kp, vp, sp, ks_, vs_, ss_ = [], [], [], [], [], []
    for l in range(DEPTH):
        lam_init = 0.8 - 0.6 * math.exp(-0.3 * l)
        w = (norm_f1[l], w_up1[l], w_down1[l], norm_mix[l], w_in[l],
             lambda_q1[l], lambda_k1[l], lambda_q2[l], lambda_k2[l], da_norm[l], ret_norm[l],
             w_a_proj[l], w_r_proj[l], w_o[l], norm_f2[l], w_up2[l], w_down2[l], w_ada[l], b_ada[l])
        hp, k_new_p, v_new_p, s_new_p = layer_step(hp, c_prompt, None, None, None, lam_init, *w)
        hs, k_new_s, v_new_s, s_new_s = layer_step(hs, c_sample, cache_k[l], cache_v[l], state_ret[l], lam_init, *w)
        kp.append(k_new_p); vp.append(v_new_p); sp.append(s_new_p)
        ks_.append(k_new_s); vs_.append(v_new_s); ss_.append(s_new_s)
    y_prompt = rms_norm(hp, norm_final)
    y_sample = rms_norm(hs, norm_final)
    return (y_prompt, y_sample, jnp.stack(kp), jnp.stack(vp), jnp.stack(sp),
            jnp.stack(ks_), jnp.stack(vs_), jnp.stack(ss_))
```

```python
import functools
import math

import jax
import jax.numpy as jnp
from jax import lax
from jax.experimental import pallas as pl
from jax.experimental.pallas import tpu as pltpu

F32 = jnp.float32
BF16 = jnp.bfloat16

D_MODEL = 1024
D_FF = 2816
N_MOD = 9
CHUNK = 64
DA_HEADS = 4
DA_HD = 64
DA_W = 512
HEAD_W = 128
RET_HEADS = 4
RET_QK = 64
RET_QK_W = 256
RET_VD = 128
RET_V_W = 512
ROPE_BASE = 10000.0
EPS = 1e-6
NEG_INF = -1e30
LAM_INIT = 0.8 - 0.6 * math.exp(-0.3 * 0)

_IN_SIZES = (DA_W, DA_W, DA_W, RET_QK_W, RET_QK_W, RET_V_W, RET_V_W, D_MODEL, D_MODEL)
_IN_OFF = [0]
for _s in _IN_SIZES:
    _IN_OFF.append(_IN_OFF[-1] + _s)
IN_WIDTH = _IN_OFF[-1]

TOKEN_TILE = 512
ATTN_TQ = 256
SAMPLE_TK = 1024
RET_BLOCK = 256
VMEM_LIMIT = 56 * 1024 * 1024


def _const_spec(shape):
    nd = len(shape)
    return pl.BlockSpec(shape, lambda *_: (0,) * nd, pipeline_mode=pl.Buffered(1))


def _params(*sem):
    return pltpu.CompilerParams(dimension_semantics=sem, vmem_limit_bytes=VMEM_LIMIT)


def _rms(x, w):
    ms = jnp.mean(x * x, axis=-1, keepdims=True)
    return x * lax.rsqrt(ms + EPS) * w


def _silu(x):
    return x * jax.nn.sigmoid(x)


def _dot(a, b):
    return jnp.dot(a, b, preferred_element_type=F32)


def _dot_nt(a, b):
    return lax.dot_general(a, b, (((1,), (1,)), ((), ())), preferred_element_type=F32)


def _dot_tn(a, b):
    return lax.dot_general(a, b, (((0,), (0,)), ((), ())), preferred_element_type=F32)


def _mod_kernel(c_ref, w_ref, b_ref, o_ref):
    s = _silu(c_ref[...]).astype(BF16)
    o_ref[...] = _dot(s, w_ref[...].astype(BF16)) + b_ref[...]


def _modulation(c, w_ada, b_ada):
    n, width = c.shape[0], w_ada.shape[1]
    tn = width // 8
    return pl.pallas_call(
        _mod_kernel,
        grid=(width // tn,),
        in_specs=[_const_spec((n, D_MODEL)),
                  pl.BlockSpec((D_MODEL, tn), lambda j: (0, j)),
                  pl.BlockSpec((1, tn), lambda j: (0, j))],
        out_specs=pl.BlockSpec((n, tn), lambda j: (0, j)),
        out_shape=jax.ShapeDtypeStruct((n, width), F32),
        compiler_params=_params("parallel"),
        name="mod",
    )(c, w_ada, b_ada.reshape(1, width))


def _token_grid(x):
    bsz, seq, _ = x.shape
    ln = min(seq, TOKEN_TILE)
    nb = TOKEN_TILE // ln
    assert seq % ln == 0 and bsz % nb == 0
    return nb, ln, (bsz // nb, seq // ln)


def _tok_spec(nb, ln, width):
    return pl.BlockSpec((nb, ln, width), lambda b, s: (b, s, 0))


def _seq_spec(nb):
    return pl.BlockSpec((nb, 1, D_MODEL), lambda b, s: (b, 0, 0))


def _ffn_kernel(*refs, final):
    if final:
        x_ref, sh_ref, sc_ref, g_ref, nw_ref, wup_ref, wdn_ref, nf_ref, o_ref = refs
    else:
        x_ref, sh_ref, sc_ref, g_ref, nw_ref, wup_ref, wdn_ref, o_ref = refs
    x = x_ref[...]
    nb, ln, _ = x.shape
    h = _rms(x, nw_ref[...]) * (1.0 + sc_ref[...]) + sh_ref[...]
    hb = h.reshape(nb * ln, D_MODEL).astype(BF16)
    a = _dot(hb, wup_ref[:, :D_FF])
    b = _dot(hb, wup_ref[:, D_FF:])
    act = (_silu(a) * b).astype(BF16)
    y = _dot(act, wdn_ref[...]).reshape(nb, ln, D_MODEL)
    out = x + 0.5 * g_ref[...] * y
    if final:
        out = _rms(out, nf_ref[...])
    o_ref[...] = out


def _ffn(x, shift, scale, gate, norm_w, w_up, w_down, norm_final=None):
    nb, ln, grid = _token_grid(x)
    final = norm_final is not None
    in_specs = [_tok_spec(nb, ln, D_MODEL), _seq_spec(nb), _seq_spec(nb), _seq_spec(nb),
                _const_spec((1, D_MODEL)), _const_spec(w_up.shape), _const_spec(w_down.shape)]
    args = [x, shift, scale, gate, norm_w, w_up, w_down]
    if final:
        in_specs.append(_const_spec((1, D_MODEL)))
        args.append(norm_final)
    return pl.pallas_call(
        functools.partial(_ffn_kernel, final=final),
        grid=grid,
        in_specs=in_specs,
        out_specs=_tok_spec(nb, ln, D_MODEL),
        out_shape=jax.ShapeDtypeStruct(x.shape, F32),
        compiler_params=_params("parallel", "parallel"),
        name="ffn_final" if final else "ffn",
    )(*args)


def _rotary(x, cos, sin_signed):
    lane = lax.broadcasted_iota(jnp.int32, x.shape, 1)
    first_half = (lane % RET_QK) < (RET_QK // 2)
    width = x.shape[1]
    partner = jnp.where(first_half, pltpu.roll(x, width - RET_QK // 2, 1), pltpu.roll(x, RET_QK // 2, 1))
    return x * cos + partner * sin_signed


def _proj_kernel(x_ref, sh_ref, sc_ref, nw_ref, win_ref, cos_ref, sin_ref,
                 qa_ref, ka_ref, kab_ref, va_ref, vab_ref, qr_ref, kr_ref, vr_ref, zr_ref, ga_ref, gr_ref):
    x = x_ref[...]
    nb, ln, _ = x.shape
    h = _rms(x, nw_ref[...]) * (1.0 + sc_ref[...]) + sh_ref[...]
    hb = h.reshape(nb * ln, D_MODEL).astype(BF16)

    def proj(i):
        return _dot(hb, win_ref[:, _IN_OFF[i]:_IN_OFF[i + 1]])

    def put(ref, val):
        ref[...] = val.reshape(nb, ln, val.shape[-1]).astype(ref.dtype)

    put(qa_ref, proj(0) * (DA_HD ** -0.5))
    ka = proj(1)
    put(ka_ref, ka)
    put(kab_ref, ka)
    va = proj(2)
    put(va_ref, va)
    put(vab_ref, va)
    cos, sin_signed = cos_ref[...], sin_ref[...]
    put(qr_ref, _rotary(proj(3), cos, sin_signed))
    put(kr_ref, _rotary(proj(4), cos, sin_signed) * (RET_QK ** -0.5))
    put(vr_ref, proj(5))
    put(zr_ref, proj(6))
    put(ga_ref, proj(7))
    put(gr_ref, proj(8))


def _rope_tables(pos, nb):
    half = RET_QK // 2
    inv = ROPE_BASE ** (-jnp.arange(half, dtype=F32) / half)
    ang = pos.astype(F32)[:, None] * inv[None, :]
    cos, sin = jnp.cos(ang), jnp.sin(ang)
    cos = jnp.tile(jnp.concatenate([cos, cos], axis=-1), (nb, RET_HEADS))
    sin = jnp.tile(jnp.concatenate([-sin, sin], axis=-1), (nb, RET_HEADS))
    return cos, sin


def _proj(x, shift, scale, norm_w, w_in, pos):
    bsz, seq, _ = x.shape
    nb, ln, grid = _token_grid(x)
    cos, sin = _rope_tables(pos, nb)
    tab_spec = pl.BlockSpec((nb * ln, RET_QK_W), lambda b, s: (s, 0))
    widths = (DA_W, DA_W, DA_W, DA_W, DA_W, RET_QK_W, RET_QK_W, RET_V_W, RET_V_W, D_MODEL, D_MODEL)
    dtypes = (BF16, F32, BF16, F32, BF16, BF16, BF16, BF16, BF16, BF16, BF16)
    return pl.pallas_call(
        _proj_kernel,
        grid=grid,
        in_specs=[_tok_spec(nb, ln, D_MODEL), _seq_spec(nb), _seq_spec(nb),
                  _const_spec((1, D_MODEL)), _const_spec(w_in.shape), tab_spec, tab_spec],
        out_specs=[_tok_spec(nb, ln, w) for w in widths],
        out_shape=[jax.ShapeDtypeStruct((bsz, seq, w), dt) for w, dt in zip(widths, dtypes)],
        compiler_params=_params("parallel", "parallel"),
        name="proj",
    )(x, shift, scale, norm_w, w_in, cos, sin)


def _lambda(lq1_ref, lk1_ref, lq2_ref, lk2_ref):
    s1 = jnp.sum(lq1_ref[...] * lk1_ref[...], axis=-1, keepdims=True)
    s2 = jnp.sum(lq2_ref[...] * lk2_ref[...], axis=-1, keepdims=True)
    return jnp.exp(s1) - jnp.exp(s2) + LAM_INIT


def _stack_maps(q):
    lane = lax.broadcasted_iota(jnp.int32, q.shape, 1)
    zero = jnp.zeros_like(q)
    return jnp.concatenate([jnp.where(lane < DA_HD, q, zero), jnp.where(lane >= DA_HD, q, zero)], axis=0)


def _softmax_step(carry, qs, kb, vb, mask):
    m, l, acc = carry
    s = _dot_nt(qs, kb)
    if mask is not None:
        s = jnp.where(mask, s, NEG_INF)
    m_new = jnp.maximum(m, jnp.max(s, axis=-1, keepdims=True))
    alpha = jnp.exp(m - m_new)
    p = jnp.exp(s - m_new)
    l = alpha * l + jnp.sum(p, axis=-1, keepdims=True)
    acc = alpha * acc + _dot(p.astype(BF16), vb)
    return m_new, l, acc


def _diff_finish(m_l_acc, lam, norm_w):
    _, l, acc = m_l_acc
    t = acc.shape[0] // 2
    o = acc / l
    o = o[:t] - lam * o[t:]
    return _rms(o, norm_w) * (1.0 - LAM_INIT)


def _attn_prompt_kernel(lq1_ref, lk1_ref, lq2_ref, lk2_ref, nw_ref, q_ref, k_ref, v_ref, o_ref):
    i = pl.program_id(2)
    tq = q_ref.shape[1]
    qs = _stack_maps(q_ref[0])
    rows = 2 * tq

    def kv(j):
        start = pl.multiple_of(j * tq, tq)
        return k_ref[0, pl.ds(start, tq), :], v_ref[0, pl.ds(start, tq), :]

    init = (jnp.full((rows, 1), NEG_INF, F32), jnp.zeros((rows, 1), F32), jnp.zeros((rows, HEAD_W), F32))
    carry = lax.fori_loop(0, i, lambda j, c: _softmax_step(c, qs, *kv(j), None), init)
    r = lax.broadcasted_iota(jnp.int32, (rows, tq), 0)
    c = lax.broadcasted_iota(jnp.int32, (rows, tq), 1)
    visible = (c // CHUNK) <= ((r % tq) // CHUNK)
    carry = _softmax_step(carry, qs, *kv(i), visible)
    lam = _lambda(lq1_ref, lk1_ref, lq2_ref, lk2_ref)
    o_ref[0] = _diff_finish(carry, lam, nw_ref[0]).astype(o_ref.dtype)


def _attn_prompt(q, k, v, lambdas, da_norm):
    bsz, seq, _ = q.shape
    tq = ATTN_TQ
    lam_spec = _const_spec((1, DA_HD))
    kv_spec = pl.BlockSpec((1, seq, HEAD_W), lambda b, h, i: (b, 0, h))
    return pl.pallas_call(
        _attn_prompt_kernel,
        grid=(bsz, DA_HEADS, seq // tq),
        in_specs=[lam_spec] * 4 + [pl.BlockSpec((1, 1, HEAD_W), lambda b, h, i: (h, 0, 0)),
                                   pl.BlockSpec((1, tq, HEAD_W), lambda b, h, i: (b, i, h)),
                                   kv_spec, kv_spec],
        out_specs=pl.BlockSpec((1, tq, HEAD_W), lambda b, h, i: (b, i, h)),
        out_shape=jax.ShapeDtypeStruct((bsz, seq, DA_W), BF16),
        compiler_params=_params("parallel", "parallel", "parallel"),
        name="attn_prompt",
    )(*lambdas, da_norm, q, k, v)


def _attn_sample_kernel(lq1_ref, lk1_ref, lq2_ref, lk2_ref, nw_ref, q_ref, kn_ref, vn_ref, kc_ref, vc_ref,
                        o_ref, m_ref, l_ref, acc_ref):
    j = pl.program_id(1)
    last = pl.num_programs(1) - 1

    @pl.when(j == 0)
    def _():
        m_ref[...] = jnp.full(m_ref.shape, NEG_INF, F32)
        l_ref[...] = jnp.zeros(l_ref.shape, F32)
        acc_ref[...] = jnp.zeros(acc_ref.shape, F32)

    def head_cols(h):
        return slice(h * HEAD_W, (h + 1) * HEAD_W)

    for h in range(DA_HEADS):
        qs = _stack_maps(q_ref[0, :, head_cols(h)])
        kb = kc_ref[0, :, head_cols(h)].astype(BF16)
        vb = vc_ref[0, :, head_cols(h)].astype(BF16)
        carry = _softmax_step((m_ref[h], l_ref[h], acc_ref[h]), qs, kb, vb, None)
        m_ref[h], l_ref[h], acc_ref[h] = carry

    @pl.when(j == last)
    def _():
        lam = _lambda(lq1_ref, lk1_ref, lq2_ref, lk2_ref)
        for h in range(DA_HEADS):
            qs = _stack_maps(q_ref[0, :, head_cols(h)])
            carry = _softmax_step((m_ref[h], l_ref[h], acc_ref[h]), qs,
                                  kn_ref[0, :, head_cols(h)], vn_ref[0, :, head_cols(h)], None)
            o_ref[0, :, head_cols(h)] = _diff_finish(carry, lam, nw_ref[h]).astype(o_ref.dtype)


def _attn_sample(q, k_new, v_new, cache_k, cache_v, lambdas, da_norm):
    bsz, ln, _ = q.shape
    past = cache_k.shape[1]
    assert past % CHUNK == 0 and ln <= CHUNK and past % SAMPLE_TK == 0
    lam_spec = _const_spec((1, DA_HD))
    new_spec = pl.BlockSpec((1, ln, DA_W), lambda b, j: (b, 0, 0))
    cache_spec = pl.BlockSpec((1, SAMPLE_TK, DA_W), lambda b, j: (b, j, 0))
    rows = 2 * ln
    return pl.pallas_call(
        _attn_sample_kernel,
        grid=(bsz, past // SAMPLE_TK),
        in_specs=[lam_spec] * 4 + [_const_spec((DA_HEADS, 1, HEAD_W)), new_spec, new_spec, new_spec,
                                   cache_spec, cache_spec],
        out_specs=new_spec,
        out_shape=jax.ShapeDtypeStruct((bsz, ln, DA_W), BF16),
        scratch_shapes=[pltpu.VMEM((DA_HEADS, rows, 1), F32), pltpu.VMEM((DA_HEADS, rows, 1), F32),
                        pltpu.VMEM((DA_HEADS, rows, HEAD_W), F32)],
        compiler_params=_params("parallel", "arbitrary"),
        name="attn_sample",
    )(*lambdas, da_norm, q, k_new, v_new, cache_k, cache_v)


def _ret_kernel(*refs, has_state):
    if has_state:
        (dec_ref, qd_ref, kd_ref, sd_ref, nw_ref, q_ref, k_ref, v_ref, z_ref, s0_ref,
         o_ref, sout_ref, st_ref) = refs
    else:
        (dec_ref, qd_ref, kd_ref, sd_ref, nw_ref, q_ref, k_ref, v_ref, z_ref,
         o_ref, sout_ref, st_ref) = refs
    c = pl.program_id(1)

    @pl.when(c == 0)
    def _():
        st_ref[...] = jnp.zeros(st_ref.shape, F32)
        if has_state:
            for h in range(RET_HEADS):
                st_ref[h * RET_QK:(h + 1) * RET_QK, h * RET_VD:(h + 1) * RET_VD] = s0_ref[0, h]

    q, k, v = q_ref[0], k_ref[0], v_ref[0]
    state = st_ref[...]
    cross = _dot(q, state.astype(BF16)) * qd_ref[...]
    lane = lax.broadcasted_iota(jnp.int32, q.shape, 1)
    outs = []
    for h in range(RET_HEADS):
        qh = jnp.where((lane >= h * RET_QK) & (lane < (h + 1) * RET_QK), q, jnp.zeros_like(q))
        scores = _dot_nt(qh, k) * dec_ref[h]
        cols = slice(h * RET_VD, (h + 1) * RET_VD)
        oh = _dot(scores.astype(BF16), v[:, cols]) + cross[:, cols]
        xc = oh - jnp.mean(oh, axis=-1, keepdims=True)
        outs.append(xc * lax.rsqrt(jnp.mean(xc * xc, axis=-1, keepdims=True) + EPS))
    o = jnp.concatenate(outs, axis=-1) * nw_ref[...]
    o_ref[0] = (o * _silu(z_ref[0].astype(F32))).astype(o_ref.dtype)

    kd = (k.astype(F32) * kd_ref[...]).astype(BF16)
    upd = _dot_tn(kd, v)
    r = lax.broadcasted_iota(jnp.int32, upd.shape, 0) // RET_QK
    cl = lax.broadcasted_iota(jnp.int32, upd.shape, 1) // RET_VD
    new_state = state * sd_ref[...] + jnp.where(r == cl, upd, 0.0)
    st_ref[...] = new_state

    @pl.when(c == pl.num_programs(1) - 1)
    def _():
        for h in range(RET_HEADS):
            sout_ref[0, h] = new_state[h * RET_QK:(h + 1) * RET_QK, h * RET_VD:(h + 1) * RET_VD]


def _retention_tables(ln):
    log_g = jnp.log(1.0 - 2.0 ** (-5.0 - jnp.arange(RET_HEADS, dtype=F32)))
    idx = jnp.arange(ln, dtype=F32)
    dist = idx[:, None] - idx[None, :]
    decay = jnp.where(dist >= 0, jnp.exp(jnp.maximum(dist, 0.0)[None] * log_g[:, None, None]), 0.0)
    q_decay = jnp.exp((idx + 1.0)[:, None] * log_g[None, :])
    k_decay = jnp.exp((ln - 1.0 - idx)[:, None] * log_g[None, :])
    s_decay = jnp.exp(ln * log_g)[None, :]
    return (decay, jnp.repeat(q_decay, RET_VD, axis=1), jnp.repeat(k_decay, RET_QK, axis=1),
            jnp.repeat(s_decay, RET_VD, axis=1))


def _retention(q, k, v, z, ret_norm, state=None):
    bsz, seq, _ = q.shape
    ln = min(seq, RET_BLOCK)
    assert seq % ln == 0
    decay, q_decay, k_decay, s_decay = _retention_tables(ln)
    has_state = state is not None

    def blk(width):
        return pl.BlockSpec((1, ln, width), lambda b, c: (b, c, 0))

    state_spec = pl.BlockSpec((1, RET_HEADS, RET_QK, RET_VD), lambda b, c: (b, 0, 0, 0))
    in_specs = [_const_spec(decay.shape), _const_spec(q_decay.shape), _const_spec(k_decay.shape),
                _const_spec(s_decay.shape), _const_spec((1, RET_V_W)),
                blk(RET_QK_W), blk(RET_QK_W), blk(RET_V_W), blk(RET_V_W)]
    args = [decay, q_decay, k_decay, s_decay, ret_norm, q, k, v, z]
    if has_state:
        in_specs.append(state_spec)
        args.append(state)
    return pl.pallas_call(
        functools.partial(_ret_kernel, has_state=has_state),
        grid=(bsz, seq // ln),
        in_specs=in_specs,
        out_specs=[blk(RET_V_W), state_spec],
        out_shape=[jax.ShapeDtypeStruct((bsz, seq, RET_V_W), BF16),
                   jax.ShapeDtypeStruct((bsz, RET_HEADS, RET_QK, RET_VD), F32)],
        scratch_shapes=[pltpu.VMEM((RET_QK_W, RET_V_W), F32)],
        compiler_params=_params("parallel", "arbitrary"),
        name="retention_state" if has_state else "retention",
    )(*args)


def _merge_kernel(x_ref, gm_ref, oa_ref, or_ref, ga_ref, gr_ref, wa_ref, wr_ref, wo_ref, o_ref):
    x = x_ref[...]
    nb, ln, _ = x.shape

    def flat(ref):
        return ref[...].reshape(nb * ln, ref.shape[-1])

    a = _dot(flat(oa_ref), wa_ref[...])
    r = _dot(flat(or_ref), wr_ref[...])
    merged = jax.nn.sigmoid(flat(ga_ref).astype(F32)) * a + jax.nn.sigmoid(flat(gr_ref).astype(F32)) * r
    y = _dot(merged.astype(BF16), wo_ref[...]).reshape(nb, ln, D_MODEL)
    o_ref[...] = x + gm_ref[...] * y


def _merge(x, gate, o_a, o_r, gate_a, gate_r, w_a, w_r, w_o):
    nb, ln, grid = _token_grid(x)
    return pl.pallas_call(
        _merge_kernel,
        grid=grid,
        in_specs=[_tok_spec(nb, ln, D_MODEL), _seq_spec(nb), _tok_spec(nb, ln, DA_W), _tok_spec(nb, ln, RET_V_W),
                  _tok_spec(nb, ln, D_MODEL), _tok_spec(nb, ln, D_MODEL),
                  _const_spec(w_a.shape), _const_spec(w_r.shape), _const_spec(w_o.shape)],
        out_specs=_tok_spec(nb, ln, D_MODEL),
        out_shape=jax.ShapeDtypeStruct(x.shape, F32),
        compiler_params=_params("parallel", "parallel"),
        name="merge",
    )(x, gate, o_a, o_r, gate_a, gate_r, w_a, w_r, w_o)


def _layer(x, mod, past, w, norm_final):
    bsz, seq, _ = x.shape
    sh1, sc1, g1, shm, scm, gm, sh2, sc2, g2 = [mod[:, i:i + 1] for i in range(N_MOD)]
    x = _ffn(x, sh1, sc1, g1, w["norm_f1"], w["w_up1"], w["w_down1"])
    offset = 0 if past is None else past[0].shape[1]
    pos = offset + jnp.arange(seq)
    q_a, k_a, k_ab, v_a, v_ab, q_r, k_r, v_r, z_r, gate_a, gate_r = _proj(
        x, shm, scm, w["norm_mix"], w["w_in"], pos)
    if past is None:
        o_a = _attn_prompt(q_a, k_ab, v_ab, w["lambdas"], w["da_norm"].reshape(DA_HEADS, 1, HEAD_W))
        o_r, s_new = _retention(q_r, k_r, v_r, z_r, w["ret_norm"])
    else:
        o_a = _attn_sample(q_a, k_ab, v_ab, past[0], past[1], w["lambdas"],
                           w["da_norm"].reshape(DA_HEADS, 1, HEAD_W))
        o_r, s_new = _retention(q_r, k_r, v_r, z_r, w["ret_norm"], past[2])
    x = _merge(x, gm, o_a, o_r, gate_a, gate_r, w["w_a_proj"], w["w_r_proj"], w["w_o"])
    x = _ffn(x, sh2, sc2, g2, w["norm_f2"], w["w_up2"], w["w_down2"], norm_final)
    k_new = k_a.reshape(1, bsz, seq, DA_HEADS, 2, DA_HD)
    v_new = v_a.reshape(1, bsz, seq, DA_HEADS, 2 * DA_HD)
    return x, k_new, v_new, s_new[None]


def kernel(x_prompt, x_sample, c_prompt, c_sample, cache_k, cache_v, state_ret, norm_f1, w_up1, w_down1, norm_mix, w_in, lambda_q1, lambda_k1, lambda_q2, lambda_k2, da_norm, ret_norm, w_a_proj, w_r_proj, w_o, norm_f2, w_up2, w_down2, w_ada, b_ada, norm_final):
    assert cache_k.shape[0] == 1, "single layer"
    n_prompt = c_prompt.shape[0]
    mod = _modulation(jnp.concatenate([c_prompt, c_sample], axis=0), w_ada[0], b_ada[0])
    mod = mod.reshape(-1, N_MOD, D_MODEL)
    w = {
        "norm_f1": norm_f1, "norm_mix": norm_mix, "norm_f2": norm_f2,
        "w_up1": w_up1[0].astype(BF16), "w_down1": w_down1[0].astype(BF16),
        "w_up2": w_up2[0].astype(BF16), "w_down2": w_down2[0].astype(BF16),
        "w_in": w_in[0].astype(BF16), "w_a_proj": w_a_proj[0].astype(BF16),
        "w_r_proj": w_r_proj[0].astype(BF16), "w_o": w_o[0].astype(BF16),
        "lambdas": (lambda_q1, lambda_k1, lambda_q2, lambda_k2),
        "da_norm": da_norm[0], "ret_norm": ret_norm[0].reshape(1, RET_V_W),
    }
    nf = norm_final.reshape(1, D_MODEL)
    dec_b, past_len = cache_k.shape[1], cache_k.shape[2]
    past = (cache_k[0].reshape(dec_b, past_len, DA_W), cache_v[0].reshape(dec_b, past_len, DA_W), state_ret[0])
    y_p, k_p, v_p, s_p = _layer(x_prompt, mod[:n_prompt], None, w, nf)
    y_s, k_s, v_s, s_s = _layer(x_sample, mod[n_prompt:], past, w, nf)
    return y_p, y_s, k_p, v_p, s_p, k_s, v_s, s_s
```

```python
import functools
import math

import jax
import jax.numpy as jnp
from jax import lax
from jax.experimental import pallas as pl
from jax.experimental.pallas import tpu as pltpu

F32 = jnp.float32
BF16 = jnp.bfloat16

D_MODEL = 1024
D_FF = 2816
N_MOD = 9
CHUNK = 64
DA_HEADS = 4
DA_HD = 64
DA_W = 512
HEAD_W = 128
RET_HEADS = 4
RET_QK = 64
RET_QK_W = 256
RET_VD = 128
RET_V_W = 512
ROPE_BASE = 10000.0
EPS = 1e-6
NEG_INF = -1e30
LAM_INIT = 0.8 - 0.6 * math.exp(-0.3 * 0)
LOG2_E = math.log2(math.e)
ONES_ROWS = 16

_IN_SIZES = (DA_W, DA_W, DA_W, RET_QK_W, RET_QK_W, RET_V_W, RET_V_W, D_MODEL, D_MODEL)
_IN_OFF = [0]
for _s in _IN_SIZES:
    _IN_OFF.append(_IN_OFF[-1] + _s)
IN_WIDTH = _IN_OFF[-1]

TOKEN_TILE = 512
ATTN_T = 256
SAMPLE_TK = 1024
RET_BLOCK = 256
VMEM_LIMIT = 56 * 1024 * 1024


def _const_spec(shape):
    nd = len(shape)
    return pl.BlockSpec(shape, lambda *_: (0,) * nd, pipeline_mode=pl.Buffered(1))


def _params(*sem):
    return pltpu.CompilerParams(dimension_semantics=sem, vmem_limit_bytes=VMEM_LIMIT)


def _rms(x, w):
    ms = jnp.mean(x * x, axis=-1, keepdims=True)
    return x * lax.rsqrt(ms + EPS) * w


def _silu(x):
    return x * jax.nn.sigmoid(x)


def _dot(a, b):
    return jnp.dot(a, b, preferred_element_type=F32)


def _dot_nt(a, b):
    return lax.dot_general(a, b, (((1,), (1,)), ((), ())), preferred_element_type=F32)


def _dot_tn(a, b):
    return lax.dot_general(a, b, (((0,), (0,)), ((), ())), preferred_element_type=F32)


def _mod_kernel(c_ref, w_ref, b_ref, o_ref):
    s = _silu(c_ref[...]).astype(BF16)
    o_ref[...] = _dot(s, w_ref[...].astype(BF16)) + b_ref[...]


def _modulation(c, w_ada, b_ada):
    n, width = c.shape[0], w_ada.shape[1]
    tn = width // 8
    return pl.pallas_call(
        _mod_kernel,
        grid=(width // tn,),
        in_specs=[_const_spec((n, D_MODEL)),
                  pl.BlockSpec((D_MODEL, tn), lambda j: (0, j)),
                  pl.BlockSpec((1, tn), lambda j: (0, j))],
        out_specs=pl.BlockSpec((n, tn), lambda j: (0, j)),
        out_shape=jax.ShapeDtypeStruct((n, width), F32),
        compiler_params=_params("parallel"),
        name="mod",
    )(c, w_ada, b_ada.reshape(1, width))


def _token_grid(x):
    bsz, seq, _ = x.shape
    ln = min(seq, TOKEN_TILE)
    nb = TOKEN_TILE // ln
    assert seq % ln == 0 and bsz % nb == 0
    return nb, ln, (bsz // nb, seq // ln)


def _tok_spec(nb, ln, width):
    return pl.BlockSpec((nb, ln, width), lambda b, s: (b, s, 0))


def _seq_spec(nb):
    return pl.BlockSpec((nb, 1, D_MODEL), lambda b, s: (b, 0, 0))


def _ffn_kernel(*refs, final):
    if final:
        x_ref, sh_ref, sc_ref, g_ref, nw_ref, wup_ref, wdn_ref, nf_ref, o_ref = refs
    else:
        x_ref, sh_ref, sc_ref, g_ref, nw_ref, wup_ref, wdn_ref, o_ref = refs
    x = x_ref[...]
    nb, ln, _ = x.shape
    h = _rms(x, nw_ref[...]) * (1.0 + sc_ref[...]) + sh_ref[...]
    hb = h.reshape(nb * ln, D_MODEL).astype(BF16)
    a = _dot(hb, wup_ref[:, :D_FF])
    b = _dot(hb, wup_ref[:, D_FF:])
    act = (_silu(a) * b).astype(BF16)
    y = _dot(act, wdn_ref[...]).reshape(nb, ln, D_MODEL)
    out = x + 0.5 * g_ref[...] * y
    if final:
        out = _rms(out, nf_ref[...])
    o_ref[...] = out


def _ffn(x, shift, scale, gate, norm_w, w_up, w_down, norm_final=None):
    nb, ln, grid = _token_grid(x)
    final = norm_final is not None
    in_specs = [_tok_spec(nb, ln, D_MODEL), _seq_spec(nb), _seq_spec(nb), _seq_spec(nb),
                _const_spec((1, D_MODEL)), _const_spec(w_up.shape), _const_spec(w_down.shape)]
    args = [x, shift, scale, gate, norm_w, w_up, w_down]
    if final:
        in_specs.append(_const_spec((1, D_MODEL)))
        args.append(norm_final)
    return pl.pallas_call(
        functools.partial(_ffn_kernel, final=final),
        grid=grid,
        in_specs=in_specs,
        out_specs=_tok_spec(nb, ln, D_MODEL),
        out_shape=jax.ShapeDtypeStruct(x.shape, F32),
        compiler_params=_params("parallel", "parallel"),
        name="ffn_final" if final else "ffn",
    )(*args)


def _rotary(x, cos, sin_signed):
    lane = lax.broadcasted_iota(jnp.int32, x.shape, 1)
    first_half = (lane % RET_QK) < (RET_QK // 2)
    width = x.shape[1]
    partner = jnp.where(first_half, pltpu.roll(x, width - RET_QK // 2, 1), pltpu.roll(x, RET_QK // 2, 1))
    return x * cos + partner * sin_signed


def _proj_kernel(x_ref, sh_ref, sc_ref, nw_ref, win_ref, cos_ref, sin_ref, *out_refs, transposed):
    a_refs, (qr_ref, kr_ref, vr_ref, zr_ref, ga_ref, gr_ref) = out_refs[:5], out_refs[5:]
    x = x_ref[...]
    nb, ln, _ = x.shape
    h = _rms(x, nw_ref[...]) * (1.0 + sc_ref[...]) + sh_ref[...]
    hb = h.reshape(nb * ln, D_MODEL).astype(BF16)

    def proj(i):
        return _dot(hb, win_ref[:, _IN_OFF[i]:_IN_OFF[i + 1]])

    def put(ref, val):
        ref[...] = val.reshape(nb, ln, val.shape[-1]).astype(ref.dtype)

    q = proj(0) * (DA_HD ** -0.5)
    k = proj(1)
    v = proj(2)
    if transposed:
        qt_ref, kb_ref, kt_ref, vt_ref, v4_ref = a_refs
        qt_ref[0] = (q * LOG2_E).T.astype(BF16)
        kb_ref[0] = k.astype(BF16)
        kt_ref[0] = k.T
        vt_ref[0] = v.T.astype(BF16)
        for hd in range(DA_HEADS):
            v4_ref[0, pl.ds(hd, ln, stride=DA_HEADS), :] = v[:, hd * HEAD_W:(hd + 1) * HEAD_W]
    else:
        q_ref, k_ref, kb_ref, v_ref, vb_ref = a_refs
        put(q_ref, q)
        put(k_ref, k)
        put(kb_ref, k)
        put(v_ref, v)
        put(vb_ref, v)
    cos, sin_signed = cos_ref[...], sin_ref[...]
    put(qr_ref, _rotary(proj(3), cos, sin_signed))
    put(kr_ref, _rotary(proj(4), cos, sin_signed) * (RET_QK ** -0.5))
    put(vr_ref, proj(5))
    put(zr_ref, proj(6))
    put(ga_ref, proj(7))
    put(gr_ref, proj(8))


def _rope_tables(pos, nb):
    half = RET_QK // 2
    inv = ROPE_BASE ** (-jnp.arange(half, dtype=F32) / half)
    ang = pos.astype(F32)[:, None] * inv[None, :]
    cos, sin = jnp.cos(ang), jnp.sin(ang)
    cos = jnp.tile(jnp.concatenate([cos, cos], axis=-1), (nb, RET_HEADS))
    sin = jnp.tile(jnp.concatenate([-sin, sin], axis=-1), (nb, RET_HEADS))
    return cos, sin


def _proj(x, shift, scale, norm_w, w_in, pos, transposed):
    bsz, seq, _ = x.shape
    nb, ln, grid = _token_grid(x)
    cos, sin = _rope_tables(pos, nb)
    tab_spec = pl.BlockSpec((nb * ln, RET_QK_W), lambda b, s: (s, 0))
    if transposed:
        assert nb == 1
        t_spec = pl.BlockSpec((1, DA_W, ln), lambda b, s: (b, 0, s))
        t_shape = (bsz, DA_W, seq)
        a_specs = [t_spec, _tok_spec(1, ln, DA_W), t_spec, t_spec,
                   pl.BlockSpec((1, DA_HEADS * ln, HEAD_W), lambda b, s: (b, s, 0))]
        a_shapes = [jax.ShapeDtypeStruct(t_shape, BF16), jax.ShapeDtypeStruct((bsz, seq, DA_W), BF16),
                    jax.ShapeDtypeStruct(t_shape, F32), jax.ShapeDtypeStruct(t_shape, BF16),
                    jax.ShapeDtypeStruct((bsz, DA_HEADS * seq, HEAD_W), F32)]
    else:
        a_specs = [_tok_spec(nb, ln, DA_W)] * 5
        a_shapes = [jax.ShapeDtypeStruct((bsz, seq, DA_W), dt) for dt in (BF16, F32, BF16, F32, BF16)]
    widths = (RET_QK_W, RET_QK_W, RET_V_W, RET_V_W, D_MODEL, D_MODEL)
    return pl.pallas_call(
        functools.partial(_proj_kernel, transposed=transposed),
        grid=grid,
        in_specs=[_tok_spec(nb, ln, D_MODEL), _seq_spec(nb), _seq_spec(nb),
                  _const_spec((1, D_MODEL)), _const_spec(w_in.shape), tab_spec, tab_spec],
        out_specs=a_specs + [_tok_spec(nb, ln, w) for w in widths],
        out_shape=a_shapes + [jax.ShapeDtypeStruct((bsz, seq, w), BF16) for w in widths],
        compiler_params=_params("parallel", "parallel"),
        name="proj_t" if transposed else "proj",
    )(x, shift, scale, norm_w, w_in, cos, sin)


def _lambda(lq1_ref, lk1_ref, lq2_ref, lk2_ref):
    s1 = jnp.sum(lq1_ref[...] * lk1_ref[...], axis=-1, keepdims=True)
    s2 = jnp.sum(lq2_ref[...] * lk2_ref[...], axis=-1, keepdims=True)
    return jnp.exp(s1) - jnp.exp(s2) + LAM_INIT


def _stack_maps(q):
    lane = lax.broadcasted_iota(jnp.int32, q.shape, 1)
    zero = jnp.zeros_like(q)
    return jnp.concatenate([jnp.where(lane < DA_HD, q, zero), jnp.where(lane >= DA_HD, q, zero)], axis=0)


def _softmax_step(carry, s, vb):
    m, l, acc = carry
    m_new = jnp.maximum(m, jnp.max(s, axis=-1, keepdims=True))
    alpha = jnp.exp(m - m_new)
    p = jnp.exp(s - m_new)
    l = alpha * l + jnp.sum(p, axis=-1, keepdims=True)
    acc = alpha * acc + _dot(p.astype(BF16), vb)
    return m_new, l, acc


def _diff_finish(m_l_acc, lam, norm_w):
    _, l, acc = m_l_acc
    t = acc.shape[0] // 2
    o = acc / l
    o = o[:t] - lam * o[t:]
    return _rms(o, norm_w) * (1.0 - LAM_INIT)


def _attn_prompt_kernel(lq1_ref, lk1_ref, lq2_ref, lk2_ref, nw_ref, qt_ref, k_ref, vt_ref, o_ref):
    seq = k_ref.shape[1]
    t = ATTN_T
    lam = _lambda(lq1_ref, lk1_ref, lq2_ref, lk2_ref)
    feat = lax.broadcasted_iota(jnp.int32, (HEAD_W, t), 0)
    key_chunk = lax.broadcasted_iota(jnp.int32, (t, t), 0) // CHUNK
    query_chunk = lax.broadcasted_iota(jnp.int32, (t, t), 1) // CHUNK
    visible = key_chunk <= query_chunk
    visible2 = jnp.concatenate([visible, visible], axis=1)

    def scores(i):
        lo, hi = i * t, (i + 1) * t
        qt = qt_ref[0, :, lo:hi]
        zero = jnp.zeros_like(qt)
        qm = jnp.concatenate([jnp.where(feat < DA_HD, qt, zero), jnp.where(feat >= DA_HD, qt, zero)], axis=1)
        s_diag = jnp.where(visible2, _dot(k_ref[0, lo:hi, :], qm), NEG_INF)
        m = jnp.max(s_diag, axis=0, keepdims=True)
        s_full = None
        if i > 0:
            s_full = _dot(k_ref[0, :lo, :], qm)
            m = jnp.maximum(m, jnp.max(s_full, axis=0, keepdims=True))
        return s_diag, s_full, m

    vt1 = jnp.concatenate([vt_ref[0], jnp.ones((ONES_ROWS, seq), BF16)], axis=0)

    def finish(i, s_diag, s_full, m):
        lo, hi = i * t, (i + 1) * t
        acc = _dot(vt1[:, lo:hi], jnp.exp2(s_diag - m).astype(BF16))
        if i > 0:
            acc = acc + _dot(vt1[:, :lo], jnp.exp2(s_full - m).astype(BF16))
        o2 = acc[:HEAD_W] / acc[HEAD_W:HEAD_W + 1]
        ot = o2[:, :t] - lam * o2[:, t:]
        o_ref[0, lo:hi, :] = (_rms(ot.T, nw_ref[0]) * (1.0 - LAM_INIT)).astype(o_ref.dtype)

    n_blocks = seq // t
    pending = scores(0)
    for i in range(1, n_blocks):
        upcoming = scores(i)
        finish(i - 1, *pending)
        pending = upcoming
    finish(n_blocks - 1, *pending)


def _attn_prompt(qt, k, vt, lambdas, da_norm):
    bsz, seq, _ = k.shape
    assert seq % ATTN_T == 0 and ATTN_T % CHUNK == 0
    lam_spec = _const_spec((1, DA_HD))
    t_spec = pl.BlockSpec((1, HEAD_W, seq), lambda b, h: (b, h, 0))
    row_spec = pl.BlockSpec((1, seq, HEAD_W), lambda b, h: (b, 0, h))
    return pl.pallas_call(
        _attn_prompt_kernel,
        grid=(bsz, DA_HEADS),
        in_specs=[lam_spec] * 4 + [pl.BlockSpec((1, 1, HEAD_W), lambda b, h: (h, 0, 0)), t_spec, row_spec, t_spec],
        out_specs=row_spec,
        out_shape=jax.ShapeDtypeStruct((bsz, seq, DA_W), BF16),
        compiler_params=_params("parallel", "parallel"),
        name="attn_prompt",
    )(*lambdas, da_norm, qt, k, vt)


def _attn_sample_kernel(lq1_ref, lk1_ref, lq2_ref, lk2_ref, nw_ref, q_ref, kn_ref, vn_ref, kc_ref, vc_ref,
                        o_ref, m_ref, l_ref, acc_ref):
    j = pl.program_id(1)
    last = pl.num_programs(1) - 1

    @pl.when(j == 0)
    def _():
        m_ref[...] = jnp.full(m_ref.shape, NEG_INF, F32)
        l_ref[...] = jnp.zeros(l_ref.shape, F32)
        acc_ref[...] = jnp.zeros(acc_ref.shape, F32)

    def head_cols(h):
        return slice(h * HEAD_W, (h + 1) * HEAD_W)

    tk = kc_ref.shape[-1]
    for h in range(DA_HEADS):
        qs = _stack_maps(q_ref[0, :, head_cols(h)])
        s = _dot(qs, kc_ref[0, h].astype(BF16))
        vb = vc_ref[0, pl.ds(h, tk, stride=DA_HEADS), :].astype(BF16)
        m_ref[h], l_ref[h], acc_ref[h] = _softmax_step((m_ref[h], l_ref[h], acc_ref[h]), s, vb)

    @pl.when(j == last)
    def _():
        lam = _lambda(lq1_ref, lk1_ref, lq2_ref, lk2_ref)
        for h in range(DA_HEADS):
            qs = _stack_maps(q_ref[0, :, head_cols(h)])
            s = _dot_nt(qs, kn_ref[0, :, head_cols(h)])
            carry = _softmax_step((m_ref[h], l_ref[h], acc_ref[h]), s, vn_ref[0, :, head_cols(h)])
            o_ref[0, :, head_cols(h)] = _diff_finish(carry, lam, nw_ref[h]).astype(o_ref.dtype)


def _attn_sample(q, k_new, v_new, cache_kt, cache_v4, lambdas, da_norm):
    bsz, ln, _ = q.shape
    past = cache_kt.shape[-1]
    assert past % CHUNK == 0 and ln <= CHUNK and past % SAMPLE_TK == 0
    lam_spec = _const_spec((1, DA_HD))
    new_spec = pl.BlockSpec((1, ln, DA_W), lambda b, j: (b, 0, 0))
    kt_spec = pl.BlockSpec((1, DA_HEADS, HEAD_W, SAMPLE_TK), lambda b, j: (b, 0, 0, j))
    v4_spec = pl.BlockSpec((1, DA_HEADS * SAMPLE_TK, HEAD_W), lambda b, j: (b, j, 0))
    rows = 2 * ln
    return pl.pallas_call(
        _attn_sample_kernel,
        grid=(bsz, past // SAMPLE_TK),
        in_specs=[lam_spec] * 4 + [_const_spec((DA_HEADS, 1, HEAD_W)), new_spec, new_spec, new_spec,
                                   kt_spec, v4_spec],
        out_specs=new_spec,
        out_shape=jax.ShapeDtypeStruct((bsz, ln, DA_W), BF16),
        scratch_shapes=[pltpu.VMEM((DA_HEADS, rows, 1), F32), pltpu.VMEM((DA_HEADS, rows, 1), F32),
                        pltpu.VMEM((DA_HEADS, rows, HEAD_W), F32)],
        compiler_params=_params("parallel", "arbitrary"),
        name="attn_sample",
    )(*lambdas, da_norm, q, k_new, v_new, cache_kt, cache_v4)


def _ret_kernel(*refs, has_state):
    if has_state:
        (dec_ref, qd_ref, kd_ref, sd_ref, nw_ref, q_ref, k_ref, v_ref, z_ref, s0_ref,
         o_ref, sout_ref, st_ref) = refs
    else:
        (dec_ref, qd_ref, kd_ref, sd_ref, nw_ref, q_ref, k_ref, v_ref, z_ref,
         o_ref, sout_ref, st_ref) = refs
    c = pl.program_id(1)

    @pl.when(c == 0)
    def _():
        st_ref[...] = jnp.zeros(st_ref.shape, F32)
        if has_state:
            for h in range(RET_HEADS):
                st_ref[h * RET_QK:(h + 1) * RET_QK, h * RET_VD:(h + 1) * RET_VD] = s0_ref[0, h]

    q, k, v = q_ref[0], k_ref[0], v_ref[0]
    state = st_ref[...]
    cross = _dot(q, state.astype(BF16)) * qd_ref[...]
    lane = lax.broadcasted_iota(jnp.int32, q.shape, 1)
    outs = []
    for h in range(RET_HEADS):
        qh = jnp.where((lane >= h * RET_QK) & (lane < (h + 1) * RET_QK), q, jnp.zeros_like(q))
        scores = _dot_nt(qh, k) * dec_ref[h]
        cols = slice(h * RET_VD, (h + 1) * RET_VD)
        oh = _dot(scores.astype(BF16), v[:, cols]) + cross[:, cols]
        xc = oh - jnp.mean(oh, axis=-1, keepdims=True)
        outs.append(xc * lax.rsqrt(jnp.mean(xc * xc, axis=-1, keepdims=True) + EPS))
    o = jnp.concatenate(outs, axis=-1) * nw_ref[...]
    o_ref[0] = (o * _silu(z_ref[0].astype(F32))).astype(o_ref.dtype)

    kd = (k.astype(F32) * kd_ref[...]).astype(BF16)
    upd = _dot_tn(kd, v)
    r = lax.broadcasted_iota(jnp.int32, upd.shape, 0) // RET_QK
    cl = lax.broadcasted_iota(jnp.int32, upd.shape, 1) // RET_VD
    new_state = state * sd_ref[...] + jnp.where(r == cl, upd, 0.0)
    st_ref[...] = new_state

    @pl.when(c == pl.num_programs(1) - 1)
    def _():
        for h in range(RET_HEADS):
            sout_ref[0, h] = new_state[h * RET_QK:(h + 1) * RET_QK, h * RET_VD:(h + 1) * RET_VD]


def _retention_tables(ln):
    log_g = jnp.log(1.0 - 2.0 ** (-5.0 - jnp.arange(RET_HEADS, dtype=F32)))
    idx = jnp.arange(ln, dtype=F32)
    dist = idx[:, None] - idx[None, :]
    decay = jnp.where(dist >= 0, jnp.exp(jnp.maximum(dist, 0.0)[None] * log_g[:, None, None]), 0.0)
    q_decay = jnp.exp((idx + 1.0)[:, None] * log_g[None, :])
    k_decay = jnp.exp((ln - 1.0 - idx)[:, None] * log_g[None, :])
    s_decay = jnp.exp(ln * log_g)[None, :]
    return (decay, jnp.repeat(q_decay, RET_VD, axis=1), jnp.repeat(k_decay, RET_QK, axis=1),
            jnp.repeat(s_decay, RET_VD, axis=1))


def _retention(q, k, v, z, ret_norm, state=None):
    bsz, seq, _ = q.shape
    ln = min(seq, RET_BLOCK)
    assert seq % ln == 0
    decay, q_decay, k_decay, s_decay = _retention_tables(ln)
    has_state = state is not None

    def blk(width):
        return pl.BlockSpec((1, ln, width), lambda b, c: (b, c, 0))

    state_spec = pl.BlockSpec((1, RET_HEADS, RET_QK, RET_VD), lambda b, c: (b, 0, 0, 0))
    in_specs = [_const_spec(decay.shape), _const_spec(q_decay.shape), _const_spec(k_decay.shape),
                _const_spec(s_decay.shape), _const_spec((1, RET_V_W)),
                blk(RET_QK_W), blk(RET_QK_W), blk(RET_V_W), blk(RET_V_W)]
    args = [decay, q_decay, k_decay, s_decay, ret_norm, q, k, v, z]
    if has_state:
        in_specs.append(state_spec)
        args.append(state)
    return pl.pallas_call(
        functools.partial(_ret_kernel, has_state=has_state),
        grid=(bsz, seq // ln),
        in_specs=in_specs,
        out_specs=[blk(RET_V_W), state_spec],
        out_shape=[jax.ShapeDtypeStruct((bsz, seq, RET_V_W), BF16),
                   jax.ShapeDtypeStruct((bsz, RET_HEADS, RET_QK, RET_VD), F32)],
        scratch_shapes=[pltpu.VMEM((RET_QK_W, RET_V_W), F32)],
        compiler_params=_params("parallel", "arbitrary"),
        name="retention_state" if has_state else "retention",
    )(*args)


def _merge_kernel(x_ref, gm_ref, oa_ref, or_ref, ga_ref, gr_ref, wa_ref, wr_ref, wo_ref, o_ref):
    x = x_ref[...]
    nb, ln, _ = x.shape

    def flat(ref):
        return ref[...].reshape(nb * ln, ref.shape[-1])

    a = _dot(flat(oa_ref), wa_ref[...])
    r = _dot(flat(or_ref), wr_ref[...])
    merged = jax.nn.sigmoid(flat(ga_ref).astype(F32)) * a + jax.nn.sigmoid(flat(gr_ref).astype(F32)) * r
    y = _dot(merged.astype(BF16), wo_ref[...]).reshape(nb, ln, D_MODEL)
    o_ref[...] = x + gm_ref[...] * y


def _merge(x, gate, o_a, o_r, gate_a, gate_r, w_a, w_r, w_o):
    nb, ln, grid = _token_grid(x)
    return pl.pallas_call(
        _merge_kernel,
        grid=grid,
        in_specs=[_tok_spec(nb, ln, D_MODEL), _seq_spec(nb), _tok_spec(nb, ln, DA_W), _tok_spec(nb, ln, RET_V_W),
                  _tok_spec(nb, ln, D_MODEL), _tok_spec(nb, ln, D_MODEL),
                  _const_spec(w_a.shape), _const_spec(w_r.shape), _const_spec(w_o.shape)],
        out_specs=_tok_spec(nb, ln, D_MODEL),
        out_shape=jax.ShapeDtypeStruct(x.shape, F32),
        compiler_params=_params("parallel", "parallel"),
        name="merge",
    )(x, gate, o_a, o_r, gate_a, gate_r, w_a, w_r, w_o)


def _layer(x, mod, past, w, norm_final):
    bsz, seq, _ = x.shape
    sh1, sc1, g1, shm, scm, gm, sh2, sc2, g2 = [mod[:, i:i + 1] for i in range(N_MOD)]
    x = _ffn(x, sh1, sc1, g1, w["norm_f1"], w["w_up1"], w["w_down1"])
    offset = 0 if past is None else past[0].shape[-1]
    pos = offset + jnp.arange(seq)
    da_norm = w["da_norm"].reshape(DA_HEADS, 1, HEAD_W)
    if past is None:
        q_t, k_b, k_t, v_t, v_4, q_r, k_r, v_r, z_r, gate_a, gate_r = _proj(
            x, shm, scm, w["norm_mix"], w["w_in"], pos, transposed=True)
        o_a = _attn_prompt(q_t, k_b, v_t, w["lambdas"], da_norm)
        o_r, s_new = _retention(q_r, k_r, v_r, z_r, w["ret_norm"])
        k_new = jnp.transpose(k_t.reshape(bsz, DA_HEADS, 2, DA_HD, seq), (0, 4, 1, 2, 3))[None]
        v_new = v_4.reshape(1, bsz, seq, DA_HEADS, HEAD_W)
    else:
        q_a, k_a, k_b, v_a, v_b, q_r, k_r, v_r, z_r, gate_a, gate_r = _proj(
            x, shm, scm, w["norm_mix"], w["w_in"], pos, transposed=False)
        o_a = _attn_sample(q_a, k_b, v_b, past[0], past[1], w["lambdas"], da_norm)
        o_r, s_new = _retention(q_r, k_r, v_r, z_r, w["ret_norm"], past[2])
        k_new = k_a.reshape(1, bsz, seq, DA_HEADS, 2, DA_HD)
        v_new = v_a.reshape(1, bsz, seq, DA_HEADS, HEAD_W)
    x = _merge(x, gm, o_a, o_r, gate_a, gate_r, w["w_a_proj"], w["w_r_proj"], w["w_o"])
    x = _ffn(x, sh2, sc2, g2, w["norm_f2"], w["w_up2"], w["w_down2"], norm_final)
    return x, k_new, v_new, s_new[None]


def kernel(x_prompt, x_sample, c_prompt, c_sample, cache_k, cache_v, state_ret, norm_f1, w_up1, w_down1, norm_mix, w_in, lambda_q1, lambda_k1, lambda_q2, lambda_k2, da_norm, ret_norm, w_a_proj, w_r_proj, w_o, norm_f2, w_up2, w_down2, w_ada, b_ada, norm_final):
    assert cache_k.shape[0] == 1, "single layer"
    n_prompt = c_prompt.shape[0]
    mod = _modulation(jnp.concatenate([c_prompt, c_sample], axis=0), w_ada[0], b_ada[0])
    mod = mod.reshape(-1, N_MOD, D_MODEL)
    w = {
        "norm_f1": norm_f1, "norm_mix": norm_mix, "norm_f2": norm_f2,
        "w_up1": w_up1[0].astype(BF16), "w_down1": w_down1[0].astype(BF16),
        "w_up2": w_up2[0].astype(BF16), "w_down2": w_down2[0].astype(BF16),
        "w_in": w_in[0].astype(BF16), "w_a_proj": w_a_proj[0].astype(BF16),
        "w_r_proj": w_r_proj[0].astype(BF16), "w_o": w_o[0].astype(BF16),
        "lambdas": (lambda_q1, lambda_k1, lambda_q2, lambda_k2),
        "da_norm": da_norm[0], "ret_norm": ret_norm[0].reshape(1, RET_V_W),
    }
    nf = norm_final.reshape(1, D_MODEL)
    dec_b, past_len = cache_k.shape[1], cache_k.shape[2]
    cache_kt = jnp.transpose(cache_k[0], (0, 2, 3, 4, 1)).reshape(dec_b, DA_HEADS, HEAD_W, past_len)
    cache_v4 = cache_v[0].reshape(dec_b, past_len * DA_HEADS, HEAD_W)
    past = (cache_kt, cache_v4, state_ret[0])
    y_p, k_p, v_p, s_p = _layer(x_prompt, mod[:n_prompt], None, w, nf)
    y_s, k_s, v_s, s_s = _layer(x_sample, mod[n_prompt:], past, w, nf)
    return y_p, y_s, k_p, v_p, s_p, k_s, v_s, s_s
```

```python
import functools
import math

import jax
import jax.numpy as jnp
from jax import lax
from jax.experimental import pallas as pl
from jax.experimental.pallas import tpu as pltpu

F32 = jnp.float32
BF16 = jnp.bfloat16

D_MODEL = 1024
D_FF = 2816
N_MOD = 9
CHUNK = 64
DA_HEADS = 4
DA_HD = 64
DA_W = 512
HEAD_W = 128
RET_HEADS = 4
RET_QK = 64
RET_QK_W = 256
RET_VD = 128
RET_V_W = 512
ROPE_BASE = 10000.0
EPS = 1e-6
NEG_INF = -1e30
LAM_INIT = 0.8 - 0.6 * math.exp(-0.3 * 0)
LOG2_E = math.log2(math.e)
ONES_ROWS = 16

_IN_SIZES = (DA_W, DA_W, DA_W, RET_QK_W, RET_QK_W, RET_V_W, RET_V_W, D_MODEL, D_MODEL)
_IN_OFF = [0]
for _s in _IN_SIZES:
    _IN_OFF.append(_IN_OFF[-1] + _s)
IN_WIDTH = _IN_OFF[-1]

TOKEN_TILE = 512
ATTN_T = 256
ATTN_HEADS_PER_STEP = 2
RET_BLOCK = 256
V7X_VMEM_BYTES = 64 * 1024 * 1024
VMEM_LIMIT = V7X_VMEM_BYTES * 7 // 8


def _const_spec(shape):
    nd = len(shape)
    return pl.BlockSpec(shape, lambda *_: (0,) * nd, pipeline_mode=pl.Buffered(1))


def _params(*sem):
    return pltpu.CompilerParams(dimension_semantics=sem, vmem_limit_bytes=VMEM_LIMIT)


def _rms(x, w):
    ms = jnp.mean(x * x, axis=-1, keepdims=True)
    return x * lax.rsqrt(ms + EPS) * w


def _silu(x):
    return x * jax.nn.sigmoid(x)


def _dot(a, b):
    return jnp.dot(a, b, preferred_element_type=F32)


def _dot_nt(a, b):
    return lax.dot_general(a, b, (((1,), (1,)), ((), ())), preferred_element_type=F32)


def _dot_tn(a, b):
    return lax.dot_general(a, b, (((0,), (0,)), ((), ())), preferred_element_type=F32)


def _mod_kernel(c_ref, w_ref, b_ref, o_ref):
    s = _silu(c_ref[...]).astype(BF16)
    o_ref[...] = _dot(s, w_ref[...].astype(BF16)) + b_ref[...]


def _modulation(c, w_ada, b_ada):
    n, width = c.shape[0], w_ada.shape[1]
    tn = width // 8
    return pl.pallas_call(
        _mod_kernel,
        grid=(width // tn,),
        in_specs=[_const_spec((n, D_MODEL)),
                  pl.BlockSpec((D_MODEL, tn), lambda j: (0, j)),
                  pl.BlockSpec((1, tn), lambda j: (0, j))],
        out_specs=pl.BlockSpec((n, tn), lambda j: (0, j)),
        out_shape=jax.ShapeDtypeStruct((n, width), F32),
        compiler_params=_params("parallel"),
        name="mod",
    )(c, w_ada, b_ada.reshape(1, width))


def _token_grid(x):
    bsz, seq, _ = x.shape
    ln = min(seq, TOKEN_TILE)
    nb = TOKEN_TILE // ln
    assert seq % ln == 0 and bsz % nb == 0
    return nb, ln, (bsz // nb, seq // ln)


def _tok_spec(nb, ln, width):
    return pl.BlockSpec((nb, ln, width), lambda b, s: (b, s, 0))


def _seq_spec(nb):
    return pl.BlockSpec((nb, 1, D_MODEL), lambda b, s: (b, 0, 0))


def _half_step(x, sh_ref, sc_ref, g_ref, nw_ref, wup_ref, wdn_ref):
    nb, ln, _ = x.shape
    h = _rms(x, nw_ref[...]) * (1.0 + sc_ref[...]) + sh_ref[...]
    hb = h.reshape(nb * ln, D_MODEL).astype(BF16)
    a = _dot(hb, wup_ref[:, :D_FF])
    b = _dot(hb, wup_ref[:, D_FF:])
    act = (_silu(a) * b).astype(BF16)
    y = _dot(act, wdn_ref[...]).reshape(nb, ln, D_MODEL)
    return x + 0.5 * g_ref[...] * y


def _ffn_kernel(x_ref, sh_ref, sc_ref, g_ref, nw_ref, wup_ref, wdn_ref, o_ref):
    o_ref[...] = _half_step(x_ref[...], sh_ref, sc_ref, g_ref, nw_ref, wup_ref, wdn_ref)


def _ffn(x, shift, scale, gate, norm_w, w_up, w_down):
    nb, ln, grid = _token_grid(x)
    return pl.pallas_call(
        _ffn_kernel,
        grid=grid,
        in_specs=[_tok_spec(nb, ln, D_MODEL), _seq_spec(nb), _seq_spec(nb), _seq_spec(nb),
                  _const_spec((1, D_MODEL)), _const_spec(w_up.shape), _const_spec(w_down.shape)],
        out_specs=_tok_spec(nb, ln, D_MODEL),
        out_shape=jax.ShapeDtypeStruct(x.shape, F32),
        compiler_params=_params("parallel", "parallel"),
        name="ffn",
    )(x, shift, scale, gate, norm_w, w_up, w_down)


def _rotary(x, cos, sin_signed):
    lane = lax.broadcasted_iota(jnp.int32, x.shape, 1)
    first_half = (lane % RET_QK) < (RET_QK // 2)
    width = x.shape[1]
    partner = jnp.where(first_half, pltpu.roll(x, width - RET_QK // 2, 1), pltpu.roll(x, RET_QK // 2, 1))
    return x * cos + partner * sin_signed


def _proj_kernel(x_ref, sh_ref, sc_ref, nw_ref, win_ref, cos_ref, sin_ref, *out_refs, transposed):
    a_refs, (qr_ref, kr_ref, vr_ref, zr_ref, ga_ref, gr_ref) = out_refs[:5], out_refs[5:]
    x = x_ref[...]
    nb, ln, _ = x.shape
    h = _rms(x, nw_ref[...]) * (1.0 + sc_ref[...]) + sh_ref[...]
    hb = h.reshape(nb * ln, D_MODEL).astype(BF16)

    def proj(i):
        return _dot(hb, win_ref[:, _IN_OFF[i]:_IN_OFF[i + 1]])

    def put(ref, val):
        ref[...] = val.reshape(nb, ln, val.shape[-1]).astype(ref.dtype)

    q = proj(0) * (DA_HD ** -0.5 * LOG2_E)
    k = proj(1)
    v = proj(2)
    if transposed:
        qt_ref, kb_ref, kt_ref, vt_ref, v4_ref = a_refs
        qt_ref[0] = q.T.astype(BF16)
        kb_ref[0] = k.astype(BF16)
        kt_ref[0] = k.T
        vt_ref[0] = v.T.astype(BF16)
        for hd in range(DA_HEADS):
            v4_ref[0, pl.ds(hd, ln, stride=DA_HEADS), :] = v[:, hd * HEAD_W:(hd + 1) * HEAD_W]
    else:
        q_ref, k_ref, kb_ref, v_ref, vb_ref = a_refs
        put(q_ref, q)
        put(k_ref, k)
        put(kb_ref, k)
        put(v_ref, v)
        put(vb_ref, v)
    cos, sin_signed = cos_ref[...], sin_ref[...]
    put(qr_ref, _rotary(proj(3), cos, sin_signed))
    put(kr_ref, _rotary(proj(4), cos, sin_signed) * (RET_QK ** -0.5))
    put(vr_ref, proj(5))
    put(zr_ref, proj(6))
    put(ga_ref, proj(7))
    put(gr_ref, proj(8))


def _rope_tables(pos, nb):
    half = RET_QK // 2
    inv = ROPE_BASE ** (-jnp.arange(half, dtype=F32) / half)
    ang = pos.astype(F32)[:, None] * inv[None, :]
    cos, sin = jnp.cos(ang), jnp.sin(ang)
    cos = jnp.tile(jnp.concatenate([cos, cos], axis=-1), (nb, RET_HEADS))
    sin = jnp.tile(jnp.concatenate([-sin, sin], axis=-1), (nb, RET_HEADS))
    return cos, sin


def _proj(x, shift, scale, norm_w, w_in, pos, transposed):
    bsz, seq, _ = x.shape
    nb, ln, grid = _token_grid(x)
    cos, sin = _rope_tables(pos, nb)
    tab_spec = pl.BlockSpec((nb * ln, RET_QK_W), lambda b, s: (s, 0))
    if transposed:
        assert nb == 1
        t_spec = pl.BlockSpec((1, DA_W, ln), lambda b, s: (b, 0, s))
        t_shape = (bsz, DA_W, seq)
        a_specs = [t_spec, _tok_spec(1, ln, DA_W), t_spec, t_spec,
                   pl.BlockSpec((1, DA_HEADS * ln, HEAD_W), lambda b, s: (b, s, 0))]
        a_shapes = [jax.ShapeDtypeStruct(t_shape, BF16), jax.ShapeDtypeStruct((bsz, seq, DA_W), BF16),
                    jax.ShapeDtypeStruct(t_shape, F32), jax.ShapeDtypeStruct(t_shape, BF16),
                    jax.ShapeDtypeStruct((bsz, DA_HEADS * seq, HEAD_W), F32)]
    else:
        a_specs = [_tok_spec(nb, ln, DA_W)] * 5
        a_shapes = [jax.ShapeDtypeStruct((bsz, seq, DA_W), dt) for dt in (BF16, F32, BF16, F32, BF16)]
    widths = (RET_QK_W, RET_QK_W, RET_V_W, RET_V_W, D_MODEL, D_MODEL)
    return pl.pallas_call(
        functools.partial(_proj_kernel, transposed=transposed),
        grid=grid,
        in_specs=[_tok_spec(nb, ln, D_MODEL), _seq_spec(nb), _seq_spec(nb),
                  _const_spec((1, D_MODEL)), _const_spec(w_in.shape), tab_spec, tab_spec],
        out_specs=a_specs + [_tok_spec(nb, ln, w) for w in widths],
        out_shape=a_shapes + [jax.ShapeDtypeStruct((bsz, seq, w), BF16) for w in widths],
        compiler_params=_params("parallel", "parallel"),
        name="proj_t" if transposed else "proj",
    )(x, shift, scale, norm_w, w_in, cos, sin)


def _lambda(lq1_ref, lk1_ref, lq2_ref, lk2_ref):
    s1 = jnp.sum(lq1_ref[...] * lk1_ref[...], axis=-1, keepdims=True)
    s2 = jnp.sum(lq2_ref[...] * lk2_ref[...], axis=-1, keepdims=True)
    return jnp.exp(s1) - jnp.exp(s2) + LAM_INIT


def _stack_maps(q):
    lane = lax.broadcasted_iota(jnp.int32, q.shape, 1)
    zero = jnp.zeros_like(q)
    return jnp.concatenate([jnp.where(lane < DA_HD, q, zero), jnp.where(lane >= DA_HD, q, zero)], axis=0)


def _attn_prompt_kernel(lq1_ref, lk1_ref, lq2_ref, lk2_ref, nw_ref, qt_ref, k_ref, vt_ref, o_ref):
    seq = k_ref.shape[1]
    heads = k_ref.shape[2] // HEAD_W
    t = ATTN_T
    lam = _lambda(lq1_ref, lk1_ref, lq2_ref, lk2_ref)
    feat = lax.broadcasted_iota(jnp.int32, (HEAD_W, t), 0)
    key_chunk = lax.broadcasted_iota(jnp.int32, (t, t), 0) // CHUNK
    query_chunk = lax.broadcasted_iota(jnp.int32, (t, t), 1) // CHUNK
    visible = key_chunk <= query_chunk
    visible2 = jnp.concatenate([visible, visible], axis=1)

    def head_cols(h):
        return slice(h * HEAD_W, (h + 1) * HEAD_W)

    def query_maps(i, h):
        qt = qt_ref[0, head_cols(h), i * t:(i + 1) * t]
        zero = jnp.zeros_like(qt)
        return jnp.concatenate([jnp.where(feat < DA_HD, qt, zero), jnp.where(feat >= DA_HD, qt, zero)], axis=1)

    def scores(i, h):
        lo, hi = i * t, (i + 1) * t
        qm = query_maps(i, h)
        s_diag = jnp.where(visible2, _dot(k_ref[0, lo:hi, head_cols(h)], qm), NEG_INF)
        m = jnp.max(s_diag, axis=0, keepdims=True)
        s_full = None
        if i > 0:
            s_full = _dot(k_ref[0, :lo, head_cols(h)], qm)
            m = jnp.maximum(m, jnp.max(s_full, axis=0, keepdims=True))
        return s_diag, s_full, m

    ones = jnp.ones((ONES_ROWS, seq), BF16)
    vt1 = [jnp.concatenate([vt_ref[0, head_cols(h), :], ones], axis=0) for h in range(heads)]

    def finish(i, h, s_diag, s_full, m):
        lo, hi = i * t, (i + 1) * t
        acc = _dot(vt1[h][:, lo:hi], jnp.exp2(s_diag - m).astype(BF16))
        if i > 0:
            acc = acc + _dot(vt1[h][:, :lo], jnp.exp2(s_full - m).astype(BF16))
        o2 = acc[:HEAD_W] / acc[HEAD_W:HEAD_W + 1]
        ot = o2[:, :t] - lam * o2[:, t:]
        o_ref[0, lo:hi, head_cols(h)] = (_rms(ot.T, nw_ref[h]) * (1.0 - LAM_INIT)).astype(o_ref.dtype)

    n_blocks = seq // t
    pending = [scores(0, h) for h in range(heads)]
    for i in range(1, n_blocks):
        upcoming = [scores(i, h) for h in range(heads)]
        for h in range(heads):
            finish(i - 1, h, *pending[h])
        pending = upcoming
    for h in range(heads):
        finish(n_blocks - 1, h, *pending[h])


def _attn_prompt(qt, k, vt, lambdas, da_norm):
    bsz, seq, _ = k.shape
    assert seq % ATTN_T == 0 and ATTN_T % CHUNK == 0
    lam_spec = _const_spec((1, DA_HD))
    g = ATTN_HEADS_PER_STEP
    t_spec = pl.BlockSpec((1, g * HEAD_W, seq), lambda b, h: (b, h, 0))
    row_spec = pl.BlockSpec((1, seq, g * HEAD_W), lambda b, h: (b, 0, h))
    return pl.pallas_call(
        _attn_prompt_kernel,
        grid=(bsz, DA_HEADS // g),
        in_specs=[lam_spec] * 4 + [pl.BlockSpec((g, 1, HEAD_W), lambda b, h: (h, 0, 0)), t_spec, row_spec, t_spec],
        out_specs=row_spec,
        out_shape=jax.ShapeDtypeStruct((bsz, seq, DA_W), BF16),
        compiler_params=_params("parallel", "parallel"),
        name="attn_prompt",
    )(*lambdas, da_norm, qt, k, vt)


def _attn_sample_kernel(lq1_ref, lk1_ref, lq2_ref, lk2_ref, nw_ref, q_ref, kn_ref, vn_ref, kc_ref, vc_ref, o_ref):
    ln = q_ref.shape[1]
    past = kc_ref.shape[-1]
    lam = _lambda(lq1_ref, lk1_ref, lq2_ref, lk2_ref)
    ones = jnp.ones((past, HEAD_W), BF16)

    def head_cols(h):
        return slice(h * HEAD_W, (h + 1) * HEAD_W)

    def scores(h):
        qs = _stack_maps(q_ref[0, :, head_cols(h)])
        s_cache = _dot(qs, kc_ref[0, h].astype(BF16))
        s_new = _dot_nt(qs, kn_ref[0, :, head_cols(h)])
        m = jnp.maximum(jnp.max(s_cache, axis=-1, keepdims=True), jnp.max(s_new, axis=-1, keepdims=True))
        return s_cache, s_new, m

    def finish(h, s_cache, s_new, m):
        v_cache = vc_ref[0, pl.ds(h, past, stride=DA_HEADS), :].astype(BF16)
        acc = _dot(jnp.exp2(s_cache - m).astype(BF16), jnp.concatenate([v_cache, ones], axis=1))
        v_new = jnp.concatenate([vn_ref[0, :, head_cols(h)], ones[:ln]], axis=1)
        acc = acc + _dot(jnp.exp2(s_new - m).astype(BF16), v_new)
        o2 = acc[:, :HEAD_W] / acc[:, HEAD_W:HEAD_W + 1]
        o = o2[:ln] - lam * o2[ln:]
        o_ref[0, :, head_cols(h)] = (_rms(o, nw_ref[h]) * (1.0 - LAM_INIT)).astype(o_ref.dtype)

    pending = scores(0)
    for h in range(1, DA_HEADS):
        upcoming = scores(h)
        finish(h - 1, *pending)
        pending = upcoming
    finish(DA_HEADS - 1, *pending)


def _attn_sample(q, k_new, v_new, cache_kt, cache_v4, lambdas, da_norm):
    bsz, ln, _ = q.shape
    past = cache_kt.shape[-1]
    assert past % CHUNK == 0 and ln <= CHUNK
    lam_spec = _const_spec((1, DA_HD))
    new_spec = pl.BlockSpec((1, ln, DA_W), lambda b: (b, 0, 0))
    return pl.pallas_call(
        _attn_sample_kernel,
        grid=(bsz,),
        in_specs=[lam_spec] * 4 + [_const_spec((DA_HEADS, 1, HEAD_W)), new_spec, new_spec, new_spec,
                                   pl.BlockSpec((1, DA_HEADS, HEAD_W, past), lambda b: (b, 0, 0, 0)),
                                   pl.BlockSpec((1, DA_HEADS * past, HEAD_W), lambda b: (b, 0, 0))],
        out_specs=new_spec,
        out_shape=jax.ShapeDtypeStruct((bsz, ln, DA_W), BF16),
        compiler_params=_params("parallel"),
        name="attn_sample",
    )(*lambdas, da_norm, q, k_new, v_new, cache_kt, cache_v4)


def _ret_kernel(*refs, has_state):
    if has_state:
        (dec_ref, qd_ref, kd_ref, sd_ref, nw_ref, q_ref, k_ref, v_ref, z_ref, s0_ref,
         o_ref, sout_ref, st_ref) = refs
    else:
        (dec_ref, qd_ref, kd_ref, sd_ref, nw_ref, q_ref, k_ref, v_ref, z_ref,
         o_ref, sout_ref, st_ref) = refs
    c = pl.program_id(1)

    @pl.when(c == 0)
    def _():
        st_ref[...] = jnp.zeros(st_ref.shape, F32)
        if has_state:
            for h in range(RET_HEADS):
                st_ref[h * RET_QK:(h + 1) * RET_QK, h * RET_VD:(h + 1) * RET_VD] = s0_ref[0, h]

    q, k, v = q_ref[0], k_ref[0], v_ref[0]
    state = st_ref[...]
    cross = _dot(q, state.astype(BF16)) * qd_ref[...]
    lane = lax.broadcasted_iota(jnp.int32, q.shape, 1)
    outs = []
    for h in range(RET_HEADS):
        qh = jnp.where((lane >= h * RET_QK) & (lane < (h + 1) * RET_QK), q, jnp.zeros_like(q))
        scores = _dot_nt(qh, k) * dec_ref[h]
        cols = slice(h * RET_VD, (h + 1) * RET_VD)
        oh = _dot(scores.astype(BF16), v[:, cols]) + cross[:, cols]
        xc = oh - jnp.mean(oh, axis=-1, keepdims=True)
        outs.append(xc * lax.rsqrt(jnp.mean(xc * xc, axis=-1, keepdims=True) + EPS))
    o = jnp.concatenate(outs, axis=-1) * nw_ref[...]
    o_ref[0] = (o * _silu(z_ref[0].astype(F32))).astype(o_ref.dtype)

    kd = (k.astype(F32) * kd_ref[...]).astype(BF16)
    upd = _dot_tn(kd, v)
    r = lax.broadcasted_iota(jnp.int32, upd.shape, 0) // RET_QK
    cl = lax.broadcasted_iota(jnp.int32, upd.shape, 1) // RET_VD
    new_state = state * sd_ref[...] + jnp.where(r == cl, upd, 0.0)
    st_ref[...] = new_state

    @pl.when(c == pl.num_programs(1) - 1)
    def _():
        for h in range(RET_HEADS):
            sout_ref[0, h] = new_state[h * RET_QK:(h + 1) * RET_QK, h * RET_VD:(h + 1) * RET_VD]


def _retention_tables(ln):
    log_g = jnp.log(1.0 - 2.0 ** (-5.0 - jnp.arange(RET_HEADS, dtype=F32)))
    idx = jnp.arange(ln, dtype=F32)
    dist = idx[:, None] - idx[None, :]
    decay = jnp.where(dist >= 0, jnp.exp(jnp.maximum(dist, 0.0)[None] * log_g[:, None, None]), 0.0)
    q_decay = jnp.exp((idx + 1.0)[:, None] * log_g[None, :])
    k_decay = jnp.exp((ln - 1.0 - idx)[:, None] * log_g[None, :])
    s_decay = jnp.exp(ln * log_g)[None, :]
    return (decay, jnp.repeat(q_decay, RET_VD, axis=1), jnp.repeat(k_decay, RET_QK, axis=1),
            jnp.repeat(s_decay, RET_VD, axis=1))


def _retention(q, k, v, z, ret_norm, state=None):
    bsz, seq, _ = q.shape
    ln = min(seq, RET_BLOCK)
    assert seq % ln == 0
    decay, q_decay, k_decay, s_decay = _retention_tables(ln)
    has_state = state is not None

    def blk(width):
        return pl.BlockSpec((1, ln, width), lambda b, c: (b, c, 0))

    state_spec = pl.BlockSpec((1, RET_HEADS, RET_QK, RET_VD), lambda b, c: (b, 0, 0, 0))
    in_specs = [_const_spec(decay.shape), _const_spec(q_decay.shape), _const_spec(k_decay.shape),
                _const_spec(s_decay.shape), _const_spec((1, RET_V_W)),
                blk(RET_QK_W), blk(RET_QK_W), blk(RET_V_W), blk(RET_V_W)]
    args = [decay, q_decay, k_decay, s_decay, ret_norm, q, k, v, z]
    if has_state:
        in_specs.append(state_spec)
        args.append(state)
    return pl.pallas_call(
        functools.partial(_ret_kernel, has_state=has_state),
        grid=(bsz, seq // ln),
        in_specs=in_specs,
        out_specs=[blk(RET_V_W), state_spec],
        out_shape=[jax.ShapeDtypeStruct((bsz, seq, RET_V_W), BF16),
                   jax.ShapeDtypeStruct((bsz, RET_HEADS, RET_QK, RET_VD), F32)],
        scratch_shapes=[pltpu.VMEM((RET_QK_W, RET_V_W), F32)],
        compiler_params=_params("parallel", "arbitrary"),
        name="retention_state" if has_state else "retention",
    )(*args)


def _merge_ffn_kernel(x_ref, gm_ref, oa_ref, or_ref, ga_ref, gr_ref, wa_ref, wr_ref, wo_ref,
                      sh_ref, sc_ref, g_ref, nw_ref, wup_ref, wdn_ref, nf_ref, o_ref):
    x = x_ref[...]
    nb, ln, _ = x.shape

    def flat(ref):
        return ref[...].reshape(nb * ln, ref.shape[-1])

    a = _dot(flat(oa_ref), wa_ref[...])
    r = _dot(flat(or_ref), wr_ref[...])
    merged = jax.nn.sigmoid(flat(ga_ref).astype(F32)) * a + jax.nn.sigmoid(flat(gr_ref).astype(F32)) * r
    y = _dot(merged.astype(BF16), wo_ref[...]).reshape(nb, ln, D_MODEL)
    x = x + gm_ref[...] * y
    x = _half_step(x, sh_ref, sc_ref, g_ref, nw_ref, wup_ref, wdn_ref)
    o_ref[...] = _rms(x, nf_ref[...])


def _merge_ffn(x, gate_m, o_a, o_r, gate_a, gate_r, w_a, w_r, w_o, shift, scale, gate, norm_w, w_up, w_down,
               norm_final):
    nb, ln, grid = _token_grid(x)
    return pl.pallas_call(
        _merge_ffn_kernel,
        grid=grid,
        in_specs=[_tok_spec(nb, ln, D_MODEL), _seq_spec(nb), _tok_spec(nb, ln, DA_W), _tok_spec(nb, ln, RET_V_W),
                  _tok_spec(nb, ln, D_MODEL), _tok_spec(nb, ln, D_MODEL),
                  _const_spec(w_a.shape), _const_spec(w_r.shape), _const_spec(w_o.shape),
                  _seq_spec(nb), _seq_spec(nb), _seq_spec(nb),
                  _const_spec((1, D_MODEL)), _const_spec(w_up.shape), _const_spec(w_down.shape),
                  _const_spec((1, D_MODEL))],
        out_specs=_tok_spec(nb, ln, D_MODEL),
        out_shape=jax.ShapeDtypeStruct(x.shape, F32),
        compiler_params=_params("parallel", "parallel"),
        name="merge_ffn",
    )(x, gate_m, o_a, o_r, gate_a, gate_r, w_a, w_r, w_o, shift, scale, gate, norm_w, w_up, w_down, norm_final)


def _layer(x, mod, past, w, norm_final):
    bsz, seq, _ = x.shape
    sh1, sc1, g1, shm, scm, gm, sh2, sc2, g2 = [mod[:, i:i + 1] for i in range(N_MOD)]
    x = _ffn(x, sh1, sc1, g1, w["norm_f1"], w["w_up1"], w["w_down1"])
    offset = 0 if past is None else past[0].shape[-1]
    pos = offset + jnp.arange(seq)
    da_norm = w["da_norm"].reshape(DA_HEADS, 1, HEAD_W)
    if past is None:
        q_t, k_b, k_t, v_t, v_4, q_r, k_r, v_r, z_r, gate_a, gate_r = _proj(
            x, shm, scm, w["norm_mix"], w["w_in"], pos, transposed=True)
        o_a = _attn_prompt(q_t, k_b, v_t, w["lambdas"], da_norm)
        o_r, s_new = _retention(q_r, k_r, v_r, z_r, w["ret_norm"])
        k_new = jnp.transpose(k_t.reshape(bsz, DA_HEADS, 2, DA_HD, seq), (0, 4, 1, 2, 3))[None]
        v_new = v_4.reshape(1, bsz, seq, DA_HEADS, HEAD_W)
    else:
        q_a, k_a, k_b, v_a, v_b, q_r, k_r, v_r, z_r, gate_a, gate_r = _proj(
            x, shm, scm, w["norm_mix"], w["w_in"], pos, transposed=False)
        o_a = _attn_sample(q_a, k_b, v_b, past[0], past[1], w["lambdas"], da_norm)
        o_r, s_new = _retention(q_r, k_r, v_r, z_r, w["ret_norm"], past[2])
        k_new = k_a.reshape(1, bsz, seq, DA_HEADS, 2, DA_HD)
        v_new = v_a.reshape(1, bsz, seq, DA_HEADS, HEAD_W)
    x = _merge_ffn(x, gm, o_a, o_r, gate_a, gate_r, w["w_a_proj"], w["w_r_proj"], w["w_o"],
                   sh2, sc2, g2, w["norm_f2"], w["w_up2"], w["w_down2"], norm_final)
    return x, k_new, v_new, s_new[None]


def kernel(x_prompt, x_sample, c_prompt, c_sample, cache_k, cache_v, state_ret, norm_f1, w_up1, w_down1, norm_mix, w_in, lambda_q1, lambda_k1, lambda_q2, lambda_k2, da_norm, ret_norm, w_a_proj, w_r_proj, w_o, norm_f2, w_up2, w_down2, w_ada, b_ada, norm_final):
    assert cache_k.shape[0] == 1, "single layer"
    n_prompt = c_prompt.shape[0]
    mod = _modulation(jnp.concatenate([c_prompt, c_sample], axis=0), w_ada[0], b_ada[0])
    mod = mod.reshape(-1, N_MOD, D_MODEL)
    w = {
        "norm_f1": norm_f1, "norm_mix": norm_mix, "norm_f2": norm_f2,
        "w_up1": w_up1[0].astype(BF16), "w_down1": w_down1[0].astype(BF16),
        "w_up2": w_up2[0].astype(BF16), "w_down2": w_down2[0].astype(BF16),
        "w_in": w_in[0].astype(BF16), "w_a_proj": w_a_proj[0].astype(BF16),
        "w_r_proj": w_r_proj[0].astype(BF16), "w_o": w_o[0].astype(BF16),
        "lambdas": (lambda_q1, lambda_k1, lambda_q2, lambda_k2),
        "da_norm": da_norm[0], "ret_norm": ret_norm[0].reshape(1, RET_V_W),
    }
    nf = norm_final.reshape(1, D_MODEL)
    dec_b, past_len = cache_k.shape[1], cache_k.shape[2]
    cache_kt = jnp.transpose(cache_k[0], (0, 2, 3, 4, 1)).reshape(dec_b, DA_HEADS, HEAD_W, past_len)
    cache_v4 = cache_v[0].reshape(dec_b, past_len * DA_HEADS, HEAD_W)
    past = (cache_kt, cache_v4, state_ret[0])
    y_p, k_p, v_p, s_p = _layer(x_prompt, mod[:n_prompt], None, w, nf)
    y_s, k_s, v_s, s_s = _layer(x_sample, mod[n_prompt:], past, w, nf)
    return y_p, y_s, k_p, v_p, s_p, k_s, v_s, s_s
```

```python
import functools
import math

import jax
import jax.numpy as jnp
from jax import lax
from jax.experimental import pallas as pl
from jax.experimental.pallas import tpu as pltpu

F32 = jnp.float32
BF16 = jnp.bfloat16

D_MODEL = 1024
D_FF = 2816
N_MOD = 9
CHUNK = 64
DA_HEADS = 4
DA_HD = 64
DA_W = 512
HEAD_W = 128
RET_HEADS = 4
RET_QK = 64
RET_QK_W = 256
RET_VD = 128
RET_V_W = 512
ROPE_BASE = 10000.0
EPS = 1e-6
NEG_INF = -1e30
LAM_INIT = 0.8 - 0.6 * math.exp(-0.3 * 0)
LOG2_E = math.log2(math.e)
ONES_ROWS = 16

_IN_SIZES = (DA_W, DA_W, DA_W, RET_QK_W, RET_QK_W, RET_V_W, RET_V_W, D_MODEL, D_MODEL)
_IN_OFF = [0]
for _s in _IN_SIZES:
    _IN_OFF.append(_IN_OFF[-1] + _s)
IN_WIDTH = _IN_OFF[-1]

TOKEN_TILE = 512
ATTN_T = 256
ATTN_HEADS_PER_STEP = 2
RET_BLOCK = 256
V7X_VMEM_BYTES = 64 * 1024 * 1024
VMEM_LIMIT = V7X_VMEM_BYTES * 7 // 8


def _const_spec(shape):
    nd = len(shape)
    return pl.BlockSpec(shape, lambda *_: (0,) * nd, pipeline_mode=pl.Buffered(1))


def _params(*sem):
    return pltpu.CompilerParams(dimension_semantics=sem, vmem_limit_bytes=VMEM_LIMIT)


def _rms(x, w):
    ms = jnp.mean(x * x, axis=-1, keepdims=True)
    return x * lax.rsqrt(ms + EPS) * w


def _silu(x):
    return x * jax.nn.sigmoid(x)


def _dot(a, b):
    return jnp.dot(a, b, preferred_element_type=F32)


def _dot_nt(a, b):
    return lax.dot_general(a, b, (((1,), (1,)), ((), ())), preferred_element_type=F32)


def _dot_tn(a, b):
    return lax.dot_general(a, b, (((0,), (0,)), ((), ())), preferred_element_type=F32)


def _mod_kernel(c_ref, w_ref, b_ref, o_ref):
    s = _silu(c_ref[...]).astype(BF16)
    o_ref[...] = _dot(s, w_ref[...].astype(BF16)) + b_ref[...]


def _modulation(c, w_ada, b_ada):
    n, width = c.shape[0], w_ada.shape[1]
    tn = width // 8
    return pl.pallas_call(
        _mod_kernel,
        grid=(width // tn,),
        in_specs=[_const_spec((n, D_MODEL)),
                  pl.BlockSpec((D_MODEL, tn), lambda j: (0, j)),
                  pl.BlockSpec((1, tn), lambda j: (0, j))],
        out_specs=pl.BlockSpec((n, tn), lambda j: (0, j)),
        out_shape=jax.ShapeDtypeStruct((n, width), F32),
        compiler_params=_params("parallel"),
        name="mod",
    )(c, w_ada, b_ada.reshape(1, width))


def _token_grid(x):
    bsz, seq, _ = x.shape
    ln = min(seq, TOKEN_TILE)
    nb = TOKEN_TILE // ln
    assert seq % ln == 0 and bsz % nb == 0
    return nb, ln, (bsz // nb, seq // ln)


def _tok_spec(nb, ln, width):
    return pl.BlockSpec((nb, ln, width), lambda b, s: (b, s, 0))


def _seq_spec(nb):
    return pl.BlockSpec((nb, 1, D_MODEL), lambda b, s: (b, 0, 0))


def _half_step(x, sh_ref, sc_ref, g_ref, nw_ref, wup_ref, wdn_ref):
    nb, ln, _ = x.shape
    h = _rms(x, nw_ref[...]) * (1.0 + sc_ref[...]) + sh_ref[...]
    hb = h.reshape(nb * ln, D_MODEL).astype(BF16)
    a = _dot(hb, wup_ref[:, :D_FF])
    b = _dot(hb, wup_ref[:, D_FF:])
    act = (_silu(a) * b).astype(BF16)
    y = _dot(act, wdn_ref[...]).reshape(nb, ln, D_MODEL)
    return x + 0.5 * g_ref[...] * y


def _ffn_kernel(x_ref, sh_ref, sc_ref, g_ref, nw_ref, wup_ref, wdn_ref, o_ref):
    o_ref[...] = _half_step(x_ref[...], sh_ref, sc_ref, g_ref, nw_ref, wup_ref, wdn_ref)


def _ffn(x, shift, scale, gate, norm_w, w_up, w_down):
    nb, ln, grid = _token_grid(x)
    return pl.pallas_call(
        _ffn_kernel,
        grid=grid,
        in_specs=[_tok_spec(nb, ln, D_MODEL), _seq_spec(nb), _seq_spec(nb), _seq_spec(nb),
                  _const_spec((1, D_MODEL)), _const_spec(w_up.shape), _const_spec(w_down.shape)],
        out_specs=_tok_spec(nb, ln, D_MODEL),
        out_shape=jax.ShapeDtypeStruct(x.shape, F32),
        compiler_params=_params("parallel", "parallel"),
        name="ffn",
    )(x, shift, scale, gate, norm_w, w_up, w_down)


def _rotary(x, cos, sin_signed):
    lane = lax.broadcasted_iota(jnp.int32, x.shape, 1)
    first_half = (lane % RET_QK) < (RET_QK // 2)
    width = x.shape[1]
    partner = jnp.where(first_half, pltpu.roll(x, width - RET_QK // 2, 1), pltpu.roll(x, RET_QK // 2, 1))
    return x * cos + partner * sin_signed


def _proj_kernel(*refs, transposed, has_state):
    (x_ref, sh_ref, sc_ref, nw_ref, win_ref, cos_ref, sin_ref,
     dec_ref, qd_ref, kd_ref, sd_ref, rn_ref), refs = refs[:12], refs[12:]
    if has_state:
        s0_ref, refs = refs[0], refs[1:]
    a_refs, (or_ref, ga_ref, gr_ref, sout_ref, st_ref) = refs[:5], refs[5:]
    x = x_ref[...]
    nb, ln, _ = x.shape
    h = _rms(x, nw_ref[...]) * (1.0 + sc_ref[...]) + sh_ref[...]
    hb = h.reshape(nb * ln, D_MODEL).astype(BF16)

    def proj(i):
        return _dot(hb, win_ref[:, _IN_OFF[i]:_IN_OFF[i + 1]])

    def put(ref, val):
        ref[...] = val.reshape(nb, ln, val.shape[-1]).astype(ref.dtype)

    q = proj(0) * (DA_HD ** -0.5 * LOG2_E)
    k = proj(1)
    v = proj(2)
    if transposed:
        qt_ref, kb_ref, kt_ref, vt_ref, v4_ref = a_refs
        qt_ref[0] = q.T.astype(BF16)
        kb_ref[0] = k.astype(BF16)
        kt_ref[0] = k.T
        vt_ref[0] = v.T.astype(BF16)
        for hd in range(DA_HEADS):
            v4_ref[0, pl.ds(hd, ln, stride=DA_HEADS), :] = v[:, hd * HEAD_W:(hd + 1) * HEAD_W]
    else:
        q_ref, k_ref, kb_ref, v_ref, vb_ref = a_refs
        put(q_ref, q)
        put(k_ref, k)
        put(kb_ref, k)
        put(v_ref, v)
        put(vb_ref, v)
    put(ga_ref, proj(7))
    put(gr_ref, proj(8))

    seq_step = pl.program_id(1)

    @pl.when(seq_step == 0)
    def _():
        st_ref[...] = jnp.zeros(st_ref.shape, F32)
        if has_state:
            for n in range(nb):
                for hd in range(RET_HEADS):
                    st_ref[n, hd * RET_QK:(hd + 1) * RET_QK, hd * RET_VD:(hd + 1) * RET_VD] = s0_ref[n, hd]

    cos, sin_signed = cos_ref[...], sin_ref[...]
    q_r = _rotary(proj(3), cos, sin_signed).astype(BF16)
    k_r = _rotary(proj(4), cos, sin_signed) * (RET_QK ** -0.5)
    v_r = proj(5).astype(BF16)
    z_r = proj(6)
    rl = dec_ref.shape[-1]
    for n in range(nb):
        for blk in range(ln // rl):
            rows = slice(n * ln + blk * rl, n * ln + (blk + 1) * rl)
            o, st_ref[n] = _retention_block(q_r[rows], k_r[rows], v_r[rows], st_ref[n],
                                            dec_ref, qd_ref, kd_ref, sd_ref)
            o = o * rn_ref[...] * _silu(z_r[rows])
            or_ref[n, blk * rl:(blk + 1) * rl, :] = o.astype(or_ref.dtype)

    @pl.when(seq_step == pl.num_programs(1) - 1)
    def _():
        for n in range(nb):
            for hd in range(RET_HEADS):
                sout_ref[n, hd] = st_ref[n, hd * RET_QK:(hd + 1) * RET_QK, hd * RET_VD:(hd + 1) * RET_VD]


def _rope_tables(pos, nb):
    half = RET_QK // 2
    inv = ROPE_BASE ** (-jnp.arange(half, dtype=F32) / half)
    ang = pos.astype(F32)[:, None] * inv[None, :]
    cos, sin = jnp.cos(ang), jnp.sin(ang)
    cos = jnp.tile(jnp.concatenate([cos, cos], axis=-1), (nb, RET_HEADS))
    sin = jnp.tile(jnp.concatenate([-sin, sin], axis=-1), (nb, RET_HEADS))
    return cos, sin


def _proj(x, shift, scale, norm_w, w_in, pos, ret_norm, state, transposed):
    bsz, seq, _ = x.shape
    nb, ln, grid = _token_grid(x)
    cos, sin = _rope_tables(pos, nb)
    tab_spec = pl.BlockSpec((nb * ln, RET_QK_W), lambda b, s: (s, 0))
    rl = min(ln, RET_BLOCK)
    assert ln % rl == 0
    tables = _retention_tables(rl)
    has_state = state is not None
    state_spec = pl.BlockSpec((nb, RET_HEADS, RET_QK, RET_VD), lambda b, s: (b, 0, 0, 0))
    state_shape = jax.ShapeDtypeStruct((bsz, RET_HEADS, RET_QK, RET_VD), F32)
    if transposed:
        assert nb == 1
        t_spec = pl.BlockSpec((1, DA_W, ln), lambda b, s: (b, 0, s))
        t_shape = (bsz, DA_W, seq)
        a_specs = [t_spec, _tok_spec(1, ln, DA_W), t_spec, t_spec,
                   pl.BlockSpec((1, DA_HEADS * ln, HEAD_W), lambda b, s: (b, s, 0))]
        a_shapes = [jax.ShapeDtypeStruct(t_shape, BF16), jax.ShapeDtypeStruct((bsz, seq, DA_W), BF16),
                    jax.ShapeDtypeStruct(t_shape, F32), jax.ShapeDtypeStruct(t_shape, BF16),
                    jax.ShapeDtypeStruct((bsz, DA_HEADS * seq, HEAD_W), F32)]
    else:
        a_specs = [_tok_spec(nb, ln, DA_W)] * 5
        a_shapes = [jax.ShapeDtypeStruct((bsz, seq, DA_W), dt) for dt in (BF16, F32, BF16, F32, BF16)]
    widths = (RET_V_W, D_MODEL, D_MODEL)
    in_specs = [_tok_spec(nb, ln, D_MODEL), _seq_spec(nb), _seq_spec(nb),
                _const_spec((1, D_MODEL)), _const_spec(w_in.shape), tab_spec, tab_spec]
    in_specs += [_const_spec(tab.shape) for tab in tables] + [_const_spec((1, RET_V_W))]
    args = [x, shift, scale, norm_w, w_in, cos, sin, *tables, ret_norm]
    if has_state:
        in_specs.append(state_spec)
        args.append(state)
    return pl.pallas_call(
        functools.partial(_proj_kernel, transposed=transposed, has_state=has_state),
        grid=grid,
        in_specs=in_specs,
        out_specs=a_specs + [_tok_spec(nb, ln, w) for w in widths] + [state_spec],
        out_shape=a_shapes + [jax.ShapeDtypeStruct((bsz, seq, w), BF16) for w in widths] + [state_shape],
        scratch_shapes=[pltpu.VMEM((nb, RET_QK_W, RET_V_W), F32)],
        compiler_params=_params("parallel", "arbitrary"),
        name="proj_t" if transposed else "proj",
    )(*args)


def _lambda(lq1_ref, lk1_ref, lq2_ref, lk2_ref):
    s1 = jnp.sum(lq1_ref[...] * lk1_ref[...], axis=-1, keepdims=True)
    s2 = jnp.sum(lq2_ref[...] * lk2_ref[...], axis=-1, keepdims=True)
    return jnp.exp(s1) - jnp.exp(s2) + LAM_INIT


def _stack_maps(q):
    lane = lax.broadcasted_iota(jnp.int32, q.shape, 1)
    zero = jnp.zeros_like(q)
    return jnp.concatenate([jnp.where(lane < DA_HD, q, zero), jnp.where(lane >= DA_HD, q, zero)], axis=0)


def _attn_prompt_kernel(lq1_ref, lk1_ref, lq2_ref, lk2_ref, nw_ref, qt_ref, k_ref, vt_ref, o_ref):
    seq = k_ref.shape[1]
    heads = k_ref.shape[2] // HEAD_W
    t = ATTN_T
    lam = _lambda(lq1_ref, lk1_ref, lq2_ref, lk2_ref)
    feat = lax.broadcasted_iota(jnp.int32, (HEAD_W, t), 0)
    key_chunk = lax.broadcasted_iota(jnp.int32, (t, t), 0) // CHUNK
    query_chunk = lax.broadcasted_iota(jnp.int32, (t, t), 1) // CHUNK
    visible = key_chunk <= query_chunk
    visible2 = jnp.concatenate([visible, visible], axis=1)

    def head_cols(h):
        return slice(h * HEAD_W, (h + 1) * HEAD_W)

    def query_maps(i, h):
        qt = qt_ref[0, head_cols(h), i * t:(i + 1) * t]
        zero = jnp.zeros_like(qt)
        return jnp.concatenate([jnp.where(feat < DA_HD, qt, zero), jnp.where(feat >= DA_HD, qt, zero)], axis=1)

    def scores(i, h):
        lo, hi = i * t, (i + 1) * t
        qm = query_maps(i, h)
        s_diag = jnp.where(visible2, _dot(k_ref[0, lo:hi, head_cols(h)], qm), NEG_INF)
        m = jnp.max(s_diag, axis=0, keepdims=True)
        s_full = None
        if i > 0:
            s_full = _dot(k_ref[0, :lo, head_cols(h)], qm)
            m = jnp.maximum(m, jnp.max(s_full, axis=0, keepdims=True))
        return s_diag, s_full, m

    ones = jnp.ones((ONES_ROWS, seq), BF16)
    vt1 = [jnp.concatenate([vt_ref[0, head_cols(h), :], ones], axis=0) for h in range(heads)]

    def finish(i, h, s_diag, s_full, m):
        lo, hi = i * t, (i + 1) * t
        acc = _dot(vt1[h][:, lo:hi], jnp.exp2(s_diag - m).astype(BF16))
        if i > 0:
            acc = acc + _dot(vt1[h][:, :lo], jnp.exp2(s_full - m).astype(BF16))
        o2 = acc[:HEAD_W] / acc[HEAD_W:HEAD_W + 1]
        ot = o2[:, :t] - lam * o2[:, t:]
        o_ref[0, lo:hi, head_cols(h)] = (_rms(ot.T, nw_ref[h]) * (1.0 - LAM_INIT)).astype(o_ref.dtype)

    n_blocks = seq // t
    pending = [scores(0, h) for h in range(heads)]
    for i in range(1, n_blocks):
        upcoming = [scores(i, h) for h in range(heads)]
        for h in range(heads):
            finish(i - 1, h, *pending[h])
        pending = upcoming
    for h in range(heads):
        finish(n_blocks - 1, h, *pending[h])


def _attn_prompt(qt, k, vt, lambdas, da_norm):
    bsz, seq, _ = k.shape
    assert seq % ATTN_T == 0 and ATTN_T % CHUNK == 0
    lam_spec = _const_spec((1, DA_HD))
    g = ATTN_HEADS_PER_STEP
    t_spec = pl.BlockSpec((1, g * HEAD_W, seq), lambda b, h: (b, h, 0))
    row_spec = pl.BlockSpec((1, seq, g * HEAD_W), lambda b, h: (b, 0, h))
    return pl.pallas_call(
        _attn_prompt_kernel,
        grid=(bsz, DA_HEADS // g),
        in_specs=[lam_spec] * 4 + [pl.BlockSpec((g, 1, HEAD_W), lambda b, h: (h, 0, 0)), t_spec, row_spec, t_spec],
        out_specs=row_spec,
        out_shape=jax.ShapeDtypeStruct((bsz, seq, DA_W), BF16),
        compiler_params=_params("parallel", "parallel"),
        name="attn_prompt",
    )(*lambdas, da_norm, qt, k, vt)


def _attn_sample_kernel(lq1_ref, lk1_ref, lq2_ref, lk2_ref, nw_ref, q_ref, kn_ref, vn_ref, kc_ref, vc_ref, o_ref):
    ln = q_ref.shape[1]
    past = kc_ref.shape[-1]
    lam = _lambda(lq1_ref, lk1_ref, lq2_ref, lk2_ref)
    ones = jnp.ones((past, HEAD_W), BF16)

    def head_cols(h):
        return slice(h * HEAD_W, (h + 1) * HEAD_W)

    def scores(h):
        qs = _stack_maps(q_ref[0, :, head_cols(h)])
        s_cache = _dot(qs, kc_ref[0, h].astype(BF16))
        s_new = _dot_nt(qs, kn_ref[0, :, head_cols(h)])
        m = jnp.maximum(jnp.max(s_cache, axis=-1, keepdims=True), jnp.max(s_new, axis=-1, keepdims=True))
        return s_cache, s_new, m

    def finish(h, s_cache, s_new, m):
        v_cache = vc_ref[0, pl.ds(h, past, stride=DA_HEADS), :].astype(BF16)
        acc = _dot(jnp.exp2(s_cache - m).astype(BF16), jnp.concatenate([v_cache, ones], axis=1))
        v_new = jnp.concatenate([vn_ref[0, :, head_cols(h)], ones[:ln]], axis=1)
        acc = acc + _dot(jnp.exp2(s_new - m).astype(BF16), v_new)
        o2 = acc[:, :HEAD_W] / acc[:, HEAD_W:HEAD_W + 1]
        o = o2[:ln] - lam * o2[ln:]
        o_ref[0, :, head_cols(h)] = (_rms(o, nw_ref[h]) * (1.0 - LAM_INIT)).astype(o_ref.dtype)

    pending = scores(0)
    for h in range(1, DA_HEADS):
        upcoming = scores(h)
        finish(h - 1, *pending)
        pending = upcoming
    finish(DA_HEADS - 1, *pending)


def _attn_sample(q, k_new, v_new, cache_kt, cache_v4, lambdas, da_norm):
    bsz, ln, _ = q.shape
    past = cache_kt.shape[-1]
    assert past % CHUNK == 0 and ln <= CHUNK
    lam_spec = _const_spec((1, DA_HD))
    new_spec = pl.BlockSpec((1, ln, DA_W), lambda b: (b, 0, 0))
    return pl.pallas_call(
        _attn_sample_kernel,
        grid=(bsz,),
        in_specs=[lam_spec] * 4 + [_const_spec((DA_HEADS, 1, HEAD_W)), new_spec, new_spec, new_spec,
                                   pl.BlockSpec((1, DA_HEADS, HEAD_W, past), lambda b: (b, 0, 0, 0)),
                                   pl.BlockSpec((1, DA_HEADS * past, HEAD_W), lambda b: (b, 0, 0))],
        out_specs=new_spec,
        out_shape=jax.ShapeDtypeStruct((bsz, ln, DA_W), BF16),
        compiler_params=_params("parallel"),
        name="attn_sample",
    )(*lambdas, da_norm, q, k_new, v_new, cache_kt, cache_v4)


def _retention_block(q, k_f32, v, state, dec_ref, qd_ref, kd_ref, sd_ref):
    k = k_f32.astype(BF16)
    cross = _dot(q, state.astype(BF16)) * qd_ref[...]
    lane = lax.broadcasted_iota(jnp.int32, q.shape, 1)
    outs = []
    for h in range(RET_HEADS):
        qh = jnp.where((lane >= h * RET_QK) & (lane < (h + 1) * RET_QK), q, jnp.zeros_like(q))
        scores = _dot_nt(qh, k) * dec_ref[h]
        cols = slice(h * RET_VD, (h + 1) * RET_VD)
        oh = _dot(scores.astype(BF16), v[:, cols]) + cross[:, cols]
        xc = oh - jnp.mean(oh, axis=-1, keepdims=True)
        outs.append(xc * lax.rsqrt(jnp.mean(xc * xc, axis=-1, keepdims=True) + EPS))
    kd = (k_f32 * kd_ref[...]).astype(BF16)
    upd = _dot_tn(kd, v)
    r = lax.broadcasted_iota(jnp.int32, upd.shape, 0) // RET_QK
    cl = lax.broadcasted_iota(jnp.int32, upd.shape, 1) // RET_VD
    new_state = state * sd_ref[...] + jnp.where(r == cl, upd, 0.0)
    return jnp.concatenate(outs, axis=-1), new_state


def _retention_tables(ln):
    log_g = jnp.log(1.0 - 2.0 ** (-5.0 - jnp.arange(RET_HEADS, dtype=F32)))
    idx = jnp.arange(ln, dtype=F32)
    dist = idx[:, None] - idx[None, :]
    decay = jnp.where(dist >= 0, jnp.exp(jnp.maximum(dist, 0.0)[None] * log_g[:, None, None]), 0.0)
    q_decay = jnp.exp((idx + 1.0)[:, None] * log_g[None, :])
    k_decay = jnp.exp((ln - 1.0 - idx)[:, None] * log_g[None, :])
    s_decay = jnp.exp(ln * log_g)[None, :]
    return (decay, jnp.repeat(q_decay, RET_VD, axis=1), jnp.repeat(k_decay, RET_QK, axis=1),
            jnp.repeat(s_decay, RET_VD, axis=1))


def _merge_ffn_kernel(x_ref, gm_ref, oa_ref, or_ref, ga_ref, gr_ref, wa_ref, wr_ref, wo_ref,
                      sh_ref, sc_ref, g_ref, nw_ref, wup_ref, wdn_ref, nf_ref, o_ref):
    x = x_ref[...]
    nb, ln, _ = x.shape

    def flat(ref):
        return ref[...].reshape(nb * ln, ref.shape[-1])

    a = _dot(flat(oa_ref), wa_ref[...])
    r = _dot(flat(or_ref), wr_ref[...])
    merged = jax.nn.sigmoid(flat(ga_ref).astype(F32)) * a + jax.nn.sigmoid(flat(gr_ref).astype(F32)) * r
    y = _dot(merged.astype(BF16), wo_ref[...]).reshape(nb, ln, D_MODEL)
    x = x + gm_ref[...] * y
    x = _half_step(x, sh_ref, sc_ref, g_ref, nw_ref, wup_ref, wdn_ref)
    o_ref[...] = _rms(x, nf_ref[...])


def _merge_ffn(x, gate_m, o_a, o_r, gate_a, gate_r, w_a, w_r, w_o, shift, scale, gate, norm_w, w_up, w_down,
               norm_final):
    nb, ln, grid = _token_grid(x)
    return pl.pallas_call(
        _merge_ffn_kernel,
        grid=grid,
        in_specs=[_tok_spec(nb, ln, D_MODEL), _seq_spec(nb), _tok_spec(nb, ln, DA_W), _tok_spec(nb, ln, RET_V_W),
                  _tok_spec(nb, ln, D_MODEL), _tok_spec(nb, ln, D_MODEL),
                  _const_spec(w_a.shape), _const_spec(w_r.shape), _const_spec(w_o.shape),
                  _seq_spec(nb), _seq_spec(nb), _seq_spec(nb),
                  _const_spec((1, D_MODEL)), _const_spec(w_up.shape), _const_spec(w_down.shape),
                  _const_spec((1, D_MODEL))],
        out_specs=_tok_spec(nb, ln, D_MODEL),
        out_shape=jax.ShapeDtypeStruct(x.shape, F32),
        compiler_params=_params("parallel", "parallel"),
        name="merge_ffn",
    )(x, gate_m, o_a, o_r, gate_a, gate_r, w_a, w_r, w_o, shift, scale, gate, norm_w, w_up, w_down, norm_final)


def _layer(x, mod, past, w, norm_final):
    bsz, seq, _ = x.shape
    sh1, sc1, g1, shm, scm, gm, sh2, sc2, g2 = [mod[:, i:i + 1] for i in range(N_MOD)]
    x = _ffn(x, sh1, sc1, g1, w["norm_f1"], w["w_up1"], w["w_down1"])
    offset = 0 if past is None else past[0].shape[-1]
    pos = offset + jnp.arange(seq)
    da_norm = w["da_norm"].reshape(DA_HEADS, 1, HEAD_W)
    if past is None:
        q_t, k_b, k_t, v_t, v_4, o_r, gate_a, gate_r, s_new = _proj(
            x, shm, scm, w["norm_mix"], w["w_in"], pos, w["ret_norm"], None, transposed=True)
        o_a = _attn_prompt(q_t, k_b, v_t, w["lambdas"], da_norm)
        k_new = jnp.transpose(k_t.reshape(bsz, DA_HEADS, 2, DA_HD, seq), (0, 4, 1, 2, 3))[None]
        v_new = v_4.reshape(1, bsz, seq, DA_HEADS, HEAD_W)
    else:
        q_a, k_a, k_b, v_a, v_b, o_r, gate_a, gate_r, s_new = _proj(
            x, shm, scm, w["norm_mix"], w["w_in"], pos, w["ret_norm"], past[2], transposed=False)
        o_a = _attn_sample(q_a, k_b, v_b, past[0], past[1], w["lambdas"], da_norm)
        k_new = k_a.reshape(1, bsz, seq, DA_HEADS, 2, DA_HD)
        v_new = v_a.reshape(1, bsz, seq, DA_HEADS, HEAD_W)
    x = _merge_ffn(x, gm, o_a, o_r, gate_a, gate_r, w["w_a_proj"], w["w_r_proj"], w["w_o"],
                   sh2, sc2, g2, w["norm_f2"], w["w_up2"], w["w_down2"], norm_final)
    return x, k_new, v_new, s_new[None]


def kernel(x_prompt, x_sample, c_prompt, c_sample, cache_k, cache_v, state_ret, norm_f1, w_up1, w_down1, norm_mix, w_in, lambda_q1, lambda_k1, lambda_q2, lambda_k2, da_norm, ret_norm, w_a_proj, w_r_proj, w_o, norm_f2, w_up2, w_down2, w_ada, b_ada, norm_final):
    assert cache_k.shape[0] == 1, "single layer"
    n_prompt = c_prompt.shape[0]
    mod = _modulation(jnp.concatenate([c_prompt, c_sample], axis=0), w_ada[0], b_ada[0])
    mod = mod.reshape(-1, N_MOD, D_MODEL)
    w = {
        "norm_f1": norm_f1, "norm_mix": norm_mix, "norm_f2": norm_f2,
        "w_up1": w_up1[0].astype(BF16), "w_down1": w_down1[0].astype(BF16),
        "w_up2": w_up2[0].astype(BF16), "w_down2": w_down2[0].astype(BF16),
        "w_in": w_in[0].astype(BF16), "w_a_proj": w_a_proj[0].astype(BF16),
        "w_r_proj": w_r_proj[0].astype(BF16), "w_o": w_o[0].astype(BF16),
        "lambdas": (lambda_q1, lambda_k1, lambda_q2, lambda_k2),
        "da_norm": da_norm[0], "ret_norm": ret_norm[0].reshape(1, RET_V_W),
    }
    nf = norm_final.reshape(1, D_MODEL)
    dec_b, past_len = cache_k.shape[1], cache_k.shape[2]
    cache_kt = jnp.transpose(cache_k[0], (0, 2, 3, 4, 1)).reshape(dec_b, DA_HEADS, HEAD_W, past_len)
    cache_v4 = cache_v[0].reshape(dec_b, past_len * DA_HEADS, HEAD_W)
    past = (cache_kt, cache_v4, state_ret[0])
    y_p, k_p, v_p, s_p = _layer(x_prompt, mod[:n_prompt], None, w, nf)
    y_s, k_s, v_s, s_s = _layer(x_sample, mod[n_prompt:], past, w, nf)
    return y_p, y_s, k_p, v_p, s_p, k_s, v_s, s_s
```

```python
import functools
import math

import jax
import jax.numpy as jnp
from jax import lax
from jax.experimental import pallas as pl
from jax.experimental.pallas import tpu as pltpu

F32 = jnp.float32
BF16 = jnp.bfloat16

D_MODEL = 1024
D_FF = 2816
N_MOD = 9
CHUNK = 64
DA_HEADS = 4
DA_HD = 64
DA_W = 512
HEAD_W = 128
RET_HEADS = 4
RET_QK = 64
RET_QK_W = 256
RET_VD = 128
RET_V_W = 512
ROPE_BASE = 10000.0
EPS = 1e-6
NEG_INF = -1e30
LAM_INIT = 0.8 - 0.6 * math.exp(-0.3 * 0)
LOG2_E = math.log2(math.e)
ONES_ROWS = 16

_IN_SIZES = (DA_W, DA_W, DA_W, RET_QK_W, RET_QK_W, RET_V_W, RET_V_W, D_MODEL, D_MODEL)
_IN_OFF = [0]
for _s in _IN_SIZES:
    _IN_OFF.append(_IN_OFF[-1] + _s)
IN_WIDTH = _IN_OFF[-1]

TOKEN_TILE = 512
ATTN_T = 256
SAMPLE_HEADS_PER_STEP = 2
RET_BLOCK = 256
V7X_VMEM_BYTES = 64 * 1024 * 1024
VMEM_LIMIT = V7X_VMEM_BYTES * 7 // 8


def _const_spec(shape):
    nd = len(shape)
    return pl.BlockSpec(shape, lambda *_: (0,) * nd, pipeline_mode=pl.Buffered(1))


def _params(*sem):
    return pltpu.CompilerParams(dimension_semantics=sem, vmem_limit_bytes=VMEM_LIMIT)


def _rms(x, w):
    ms = jnp.mean(x * x, axis=-1, keepdims=True)
    return x * lax.rsqrt(ms + EPS) * w


def _silu(x):
    return x * jax.nn.sigmoid(x)


def _dot(a, b):
    return jnp.dot(a, b, preferred_element_type=F32)


def _dot_nt(a, b):
    return lax.dot_general(a, b, (((1,), (1,)), ((), ())), preferred_element_type=F32)


def _dot_tn(a, b):
    return lax.dot_general(a, b, (((0,), (0,)), ((), ())), preferred_element_type=F32)


def _mod_kernel(c_ref, w_ref, b_ref, o_ref):
    s = _silu(c_ref[...]).astype(BF16)
    o_ref[...] = _dot(s, w_ref[...].astype(BF16)) + b_ref[...]


def _modulation(c, w_ada, b_ada):
    n, width = c.shape[0], w_ada.shape[1]
    tn = width // 8
    return pl.pallas_call(
        _mod_kernel,
        grid=(width // tn,),
        in_specs=[_const_spec((n, D_MODEL)),
                  pl.BlockSpec((D_MODEL, tn), lambda j: (0, j)),
                  pl.BlockSpec((1, tn), lambda j: (0, j))],
        out_specs=pl.BlockSpec((n, tn), lambda j: (0, j)),
        out_shape=jax.ShapeDtypeStruct((n, width), F32),
        compiler_params=_params("parallel"),
        name="mod",
    )(c, w_ada, b_ada.reshape(1, width))


def _token_grid(x):
    bsz, seq, _ = x.shape
    ln = min(seq, TOKEN_TILE)
    nb = TOKEN_TILE // ln
    assert seq % ln == 0 and bsz % nb == 0
    return nb, ln, (bsz // nb, seq // ln)


def _tok_spec(nb, ln, width):
    return pl.BlockSpec((nb, ln, width), lambda b, s: (b, s, 0))


def _seq_spec(nb):
    return pl.BlockSpec((nb, 1, D_MODEL), lambda b, s: (b, 0, 0))


def _half_step(x, sh_ref, sc_ref, g_ref, nw_ref, wup_ref, wdn_ref):
    nb, ln, _ = x.shape
    h = _rms(x, nw_ref[...]) * (1.0 + sc_ref[...]) + sh_ref[...]
    hb = h.reshape(nb * ln, D_MODEL).astype(BF16)
    a = _dot(hb, wup_ref[:, :D_FF])
    b = _dot(hb, wup_ref[:, D_FF:])
    act = (_silu(a) * b).astype(BF16)
    y = _dot(act, wdn_ref[...]).reshape(nb, ln, D_MODEL)
    return x + 0.5 * g_ref[...] * y


def _ffn_kernel(x_ref, sh_ref, sc_ref, g_ref, nw_ref, wup_ref, wdn_ref, o_ref):
    o_ref[...] = _half_step(x_ref[...], sh_ref, sc_ref, g_ref, nw_ref, wup_ref, wdn_ref)


def _ffn(x, shift, scale, gate, norm_w, w_up, w_down):
    nb, ln, grid = _token_grid(x)
    return pl.pallas_call(
        _ffn_kernel,
        grid=grid,
        in_specs=[_tok_spec(nb, ln, D_MODEL), _seq_spec(nb), _seq_spec(nb), _seq_spec(nb),
                  _const_spec((1, D_MODEL)), _const_spec(w_up.shape), _const_spec(w_down.shape)],
        out_specs=_tok_spec(nb, ln, D_MODEL),
        out_shape=jax.ShapeDtypeStruct(x.shape, F32),
        compiler_params=_params("parallel", "parallel"),
        name="ffn",
    )(x, shift, scale, gate, norm_w, w_up, w_down)


def _rotary(x, cos, sin_signed):
    lane = lax.broadcasted_iota(jnp.int32, x.shape, 1)
    first_half = (lane % RET_QK) < (RET_QK // 2)
    width = x.shape[1]
    partner = jnp.where(first_half, pltpu.roll(x, width - RET_QK // 2, 1), pltpu.roll(x, RET_QK // 2, 1))
    return x * cos + partner * sin_signed


def _proj_kernel(*refs, transposed, has_state):
    (x_ref, sh_ref, sc_ref, nw_ref, win_ref, cos_ref, sin_ref,
     dec_ref, qd_ref, kd_ref, sd_ref, rn_ref), refs = refs[:12], refs[12:]
    if has_state:
        s0_ref, refs = refs[0], refs[1:]
    a_refs, (or_ref, ga_ref, gr_ref, sout_ref, st_ref) = refs[:5], refs[5:]
    x = x_ref[...]
    nb, ln, _ = x.shape
    h = _rms(x, nw_ref[...]) * (1.0 + sc_ref[...]) + sh_ref[...]
    hb = h.reshape(nb * ln, D_MODEL).astype(BF16)

    def proj(i):
        return _dot(hb, win_ref[:, _IN_OFF[i]:_IN_OFF[i + 1]])

    def put(ref, val):
        ref[...] = val.reshape(nb, ln, val.shape[-1]).astype(ref.dtype)

    q = proj(0) * (DA_HD ** -0.5 * LOG2_E)
    k = proj(1)
    v = proj(2)
    if transposed:
        qt_ref, kb_ref, kt_ref, vt_ref, v4_ref = a_refs
        qt_ref[0] = q.T.astype(BF16)
        kb_ref[0] = k.astype(BF16)
        kt_ref[0] = k.T
        vt_ref[0] = v.T.astype(BF16)
        for hd in range(DA_HEADS):
            v4_ref[0, pl.ds(hd, ln, stride=DA_HEADS), :] = v[:, hd * HEAD_W:(hd + 1) * HEAD_W]
    else:
        q_ref, k_ref, kb_ref, v_ref, vb_ref = a_refs
        put(q_ref, q)
        put(k_ref, k)
        put(kb_ref, k)
        put(v_ref, v)
        put(vb_ref, v)
    put(ga_ref, proj(7))
    put(gr_ref, proj(8))

    seq_step = pl.program_id(1)

    @pl.when(seq_step == 0)
    def _():
        st_ref[...] = jnp.zeros(st_ref.shape, F32)
        if has_state:
            for n in range(nb):
                for hd in range(RET_HEADS):
                    st_ref[n, hd * RET_QK:(hd + 1) * RET_QK, hd * RET_VD:(hd + 1) * RET_VD] = s0_ref[n, hd]

    cos, sin_signed = cos_ref[...], sin_ref[...]
    q_r = _rotary(proj(3), cos, sin_signed).astype(BF16)
    k_r = _rotary(proj(4), cos, sin_signed) * (RET_QK ** -0.5)
    v_r = proj(5).astype(BF16)
    z_r = proj(6)
    rl = dec_ref.shape[-1]
    for n in range(nb):
        for blk in range(ln // rl):
            rows = slice(n * ln + blk * rl, n * ln + (blk + 1) * rl)
            o, st_ref[n] = _retention_block(q_r[rows], k_r[rows], v_r[rows], st_ref[n],
                                            dec_ref, qd_ref, kd_ref, sd_ref)
            o = o * rn_ref[...] * _silu(z_r[rows])
            or_ref[n, blk * rl:(blk + 1) * rl, :] = o.astype(or_ref.dtype)

    @pl.when(seq_step == pl.num_programs(1) - 1)
    def _():
        for n in range(nb):
            for hd in range(RET_HEADS):
                sout_ref[n, hd] = st_ref[n, hd * RET_QK:(hd + 1) * RET_QK, hd * RET_VD:(hd + 1) * RET_VD]


def _rope_tables(pos, nb):
    half = RET_QK // 2
    inv = ROPE_BASE ** (-jnp.arange(half, dtype=F32) / half)
    ang = pos.astype(F32)[:, None] * inv[None, :]
    cos, sin = jnp.cos(ang), jnp.sin(ang)
    cos = jnp.tile(jnp.concatenate([cos, cos], axis=-1), (nb, RET_HEADS))
    sin = jnp.tile(jnp.concatenate([-sin, sin], axis=-1), (nb, RET_HEADS))
    return cos, sin


def _proj(x, shift, scale, norm_w, w_in, pos, ret_norm, state, transposed):
    bsz, seq, _ = x.shape
    nb, ln, grid = _token_grid(x)
    cos, sin = _rope_tables(pos, nb)
    tab_spec = pl.BlockSpec((nb * ln, RET_QK_W), lambda b, s: (s, 0))
    rl = min(ln, RET_BLOCK)
    assert ln % rl == 0
    tables = _retention_tables(rl)
    has_state = state is not None
    state_spec = pl.BlockSpec((nb, RET_HEADS, RET_QK, RET_VD), lambda b, s: (b, 0, 0, 0))
    state_shape = jax.ShapeDtypeStruct((bsz, RET_HEADS, RET_QK, RET_VD), F32)
    if transposed:
        assert nb == 1
        t_spec = pl.BlockSpec((1, DA_W, ln), lambda b, s: (b, 0, s))
        t_shape = (bsz, DA_W, seq)
        a_specs = [t_spec, _tok_spec(1, ln, DA_W), t_spec, t_spec,
                   pl.BlockSpec((1, DA_HEADS * ln, HEAD_W), lambda b, s: (b, s, 0))]
        a_shapes = [jax.ShapeDtypeStruct(t_shape, BF16), jax.ShapeDtypeStruct((bsz, seq, DA_W), BF16),
                    jax.ShapeDtypeStruct(t_shape, F32), jax.ShapeDtypeStruct(t_shape, BF16),
                    jax.ShapeDtypeStruct((bsz, DA_HEADS * seq, HEAD_W), F32)]
    else:
        a_specs = [_tok_spec(nb, ln, DA_W)] * 5
        a_shapes = [jax.ShapeDtypeStruct((bsz, seq, DA_W), dt) for dt in (BF16, F32, BF16, F32, BF16)]
    widths = (RET_V_W, D_MODEL, D_MODEL)
    in_specs = [_tok_spec(nb, ln, D_MODEL), _seq_spec(nb), _seq_spec(nb),
                _const_spec((1, D_MODEL)), _const_spec(w_in.shape), tab_spec, tab_spec]
    in_specs += [_const_spec(tab.shape) for tab in tables] + [_const_spec((1, RET_V_W))]
    args = [x, shift, scale, norm_w, w_in, cos, sin, *tables, ret_norm]
    if has_state:
        in_specs.append(state_spec)
        args.append(state)
    return pl.pallas_call(
        functools.partial(_proj_kernel, transposed=transposed, has_state=has_state),
        grid=grid,
        in_specs=in_specs,
        out_specs=a_specs + [_tok_spec(nb, ln, w) for w in widths] + [state_spec],
        out_shape=a_shapes + [jax.ShapeDtypeStruct((bsz, seq, w), BF16) for w in widths] + [state_shape],
        scratch_shapes=[pltpu.VMEM((nb, RET_QK_W, RET_V_W), F32)],
        compiler_params=_params("parallel", "arbitrary"),
        name="proj_t" if transposed else "proj",
    )(*args)


def _lambda(lq1_ref, lk1_ref, lq2_ref, lk2_ref):
    s1 = jnp.sum(lq1_ref[...] * lk1_ref[...], axis=-1, keepdims=True)
    s2 = jnp.sum(lq2_ref[...] * lk2_ref[...], axis=-1, keepdims=True)
    return jnp.exp(s1) - jnp.exp(s2) + LAM_INIT


def _stack_maps(q):
    lane = lax.broadcasted_iota(jnp.int32, q.shape, 1)
    zero = jnp.zeros_like(q)
    return jnp.concatenate([jnp.where(lane < DA_HD, q, zero), jnp.where(lane >= DA_HD, q, zero)], axis=0)


def _prompt_attention(lam, nw_ref, qt_ref, k_ref, vt_ref, o_ref):
    seq = k_ref.shape[1]
    heads = k_ref.shape[2] // HEAD_W
    t = ATTN_T
    feat = lax.broadcasted_iota(jnp.int32, (HEAD_W, t), 0)
    key_chunk = lax.broadcasted_iota(jnp.int32, (t, t), 0) // CHUNK
    query_chunk = lax.broadcasted_iota(jnp.int32, (t, t), 1) // CHUNK
    visible = key_chunk <= query_chunk
    visible2 = jnp.concatenate([visible, visible], axis=1)

    def head_cols(h):
        return slice(h * HEAD_W, (h + 1) * HEAD_W)

    def query_maps(i, h):
        qt = qt_ref[0, head_cols(h), i * t:(i + 1) * t]
        zero = jnp.zeros_like(qt)
        return jnp.concatenate([jnp.where(feat < DA_HD, qt, zero), jnp.where(feat >= DA_HD, qt, zero)], axis=1)

    def scores(i, h):
        lo, hi = i * t, (i + 1) * t
        qm = query_maps(i, h)
        s_diag = jnp.where(visible2, _dot(k_ref[0, lo:hi, head_cols(h)], qm), NEG_INF)
        m = jnp.max(s_diag, axis=0, keepdims=True)
        s_full = None
        if i > 0:
            s_full = _dot(k_ref[0, :lo, head_cols(h)], qm)
            m = jnp.maximum(m, jnp.max(s_full, axis=0, keepdims=True))
        return s_diag, s_full, m

    ones = jnp.ones((ONES_ROWS, seq), BF16)
    vt1 = [jnp.concatenate([vt_ref[0, head_cols(h), :], ones], axis=0) for h in range(heads)]

    def finish(i, h, s_diag, s_full, m):
        lo, hi = i * t, (i + 1) * t
        acc = _dot(vt1[h][:, lo:hi], jnp.exp2(s_diag - m).astype(BF16))
        if i > 0:
            acc = acc + _dot(vt1[h][:, :lo], jnp.exp2(s_full - m).astype(BF16))
        o2 = acc[:HEAD_W] / acc[HEAD_W:HEAD_W + 1]
        ot = o2[:, :t] - lam * o2[:, t:]
        o_ref[0, lo:hi, head_cols(h)] = (_rms(ot.T, nw_ref[h]) * (1.0 - LAM_INIT)).astype(o_ref.dtype)

    n_blocks = seq // t
    pending = [scores(0, h) for h in range(heads)]
    yield
    for i in range(1, n_blocks):
        upcoming = [scores(i, h) for h in range(heads)]
        for h in range(heads):
            finish(i - 1, h, *pending[h])
        pending = upcoming
        yield
    for h in range(heads):
        finish(n_blocks - 1, h, *pending[h])


def _sample_attention(lam, nw_ref, heads, q_ref, kn_ref, vn_ref, kc_ref, vc_ref, o_ref):
    ln = q_ref.shape[1]
    past = kc_ref.shape[-1]
    ones = jnp.ones((past, HEAD_W), BF16)

    def head_cols(j):
        return slice(j * HEAD_W, (j + 1) * HEAD_W)

    def scores(j):
        qs = _stack_maps(q_ref[0, :, head_cols(j)])
        s_cache = _dot(qs, kc_ref[0, j].astype(BF16))
        s_new = _dot_nt(qs, kn_ref[0, :, head_cols(j)])
        m = jnp.maximum(jnp.max(s_cache, axis=-1, keepdims=True), jnp.max(s_new, axis=-1, keepdims=True))
        return s_cache, s_new, m

    def finish(j, s_cache, s_new, m):
        v_cache = vc_ref[0, pl.ds(heads[j], past, stride=DA_HEADS), :].astype(BF16)
        acc = _dot(jnp.exp2(s_cache - m).astype(BF16), jnp.concatenate([v_cache, ones], axis=1))
        v_new = jnp.concatenate([vn_ref[0, :, head_cols(j)], ones[:ln]], axis=1)
        acc = acc + _dot(jnp.exp2(s_new - m).astype(BF16), v_new)
        o2 = acc[:, :HEAD_W] / acc[:, HEAD_W:HEAD_W + 1]
        o = o2[:ln] - lam * o2[ln:]
        o_ref[0, :, head_cols(j)] = (_rms(o, nw_ref[heads[j]]) * (1.0 - LAM_INIT)).astype(o_ref.dtype)

    pending = scores(0)
    yield
    for j in range(1, len(heads)):
        upcoming = scores(j)
        yield
        finish(j - 1, *pending)
        yield
        pending = upcoming
    finish(len(heads) - 1, *pending)


def _weave(main, side, main_per_side=2):
    done = object()
    main_live = side_live = True
    while main_live or side_live:
        if side_live:
            side_live = next(side, done) is not done
        for _ in range(main_per_side):
            if main_live:
                main_live = next(main, done) is not done


def _attn_kernel(lq1_ref, lk1_ref, lq2_ref, lk2_ref, nwp_ref, nws_ref, qt_ref, k_ref, vt_ref,
                 q_ref, kn_ref, vn_ref, kc_ref, vc_ref, op_ref, os_ref):
    lam = _lambda(lq1_ref, lk1_ref, lq2_ref, lk2_ref)
    g = SAMPLE_HEADS_PER_STEP
    group = pl.program_id(0) % (DA_HEADS // g)
    for grp in range(DA_HEADS // g):
        @pl.when(group == grp)
        def _(grp=grp):
            heads = tuple(range(grp * g, (grp + 1) * g))
            _weave(_prompt_attention(lam, nwp_ref, qt_ref, k_ref, vt_ref, op_ref),
                   _sample_attention(lam, nws_ref, heads, q_ref, kn_ref, vn_ref, kc_ref, vc_ref, os_ref))


def _attention(qt, k, vt, q, k_new, v_new, cache_kt, cache_v4, lambdas, da_norm):
    bp, seq, _ = k.shape
    bs, ln, _ = q.shape
    past = cache_kt.shape[-1]
    g = SAMPLE_HEADS_PER_STEP
    groups = DA_HEADS // g
    steps = bp * DA_HEADS
    assert steps == bs * groups, "one prompt (sequence, head) per sample (sequence, head group)"
    assert seq % ATTN_T == 0 and ATTN_T % CHUNK == 0 and past % CHUNK == 0 and ln <= CHUNK
    lam_spec = _const_spec((1, DA_HD))
    t_spec = pl.BlockSpec((1, HEAD_W, seq), lambda i: (i // DA_HEADS, i % DA_HEADS, 0))
    row_spec = pl.BlockSpec((1, seq, HEAD_W), lambda i: (i // DA_HEADS, 0, i % DA_HEADS))
    new_spec = pl.BlockSpec((1, ln, g * HEAD_W), lambda i: (i // groups, 0, i % groups))
    return pl.pallas_call(
        _attn_kernel,
        grid=(steps,),
        in_specs=[lam_spec] * 4 + [pl.BlockSpec((1, 1, HEAD_W), lambda i: (i % DA_HEADS, 0, 0)),
                                   _const_spec((DA_HEADS, 1, HEAD_W)), t_spec, row_spec, t_spec,
                                   new_spec, new_spec, new_spec,
                                   pl.BlockSpec((1, g, HEAD_W, past), lambda i: (i // groups, i % groups, 0, 0)),
                                   pl.BlockSpec((1, DA_HEADS * past, HEAD_W), lambda i: (i // groups, 0, 0))],
        out_specs=[row_spec, new_spec],
        out_shape=[jax.ShapeDtypeStruct((bp, seq, DA_W), BF16), jax.ShapeDtypeStruct((bs, ln, DA_W), BF16)],
        compiler_params=_params("parallel"),
        name="attn",
    )(*lambdas, da_norm, da_norm, qt, k, vt, q, k_new, v_new, cache_kt, cache_v4)


def _retention_block(q, k_f32, v, state, dec_ref, qd_ref, kd_ref, sd_ref):
    k = k_f32.astype(BF16)
    cross = _dot(q, state.astype(BF16)) * qd_ref[...]
    lane = lax.broadcasted_iota(jnp.int32, q.shape, 1)
    outs = []
    for h in range(RET_HEADS):
        qh = jnp.where((lane >= h * RET_QK) & (lane < (h + 1) * RET_QK), q, jnp.zeros_like(q))
        scores = _dot_nt(qh, k) * dec_ref[h]
        cols = slice(h * RET_VD, (h + 1) * RET_VD)
        oh = _dot(scores.astype(BF16), v[:, cols]) + cross[:, cols]
        xc = oh - jnp.mean(oh, axis=-1, keepdims=True)
        outs.append(xc * lax.rsqrt(jnp.mean(xc * xc, axis=-1, keepdims=True) + EPS))
    kd = (k_f32 * kd_ref[...]).astype(BF16)
    upd = _dot_tn(kd, v)
    r = lax.broadcasted_iota(jnp.int32, upd.shape, 0) // RET_QK
    cl = lax.broadcasted_iota(jnp.int32, upd.shape, 1) // RET_VD
    new_state = state * sd_ref[...] + jnp.where(r == cl, upd, 0.0)
    return jnp.concatenate(outs, axis=-1), new_state


def _retention_tables(ln):
    log_g = jnp.log(1.0 - 2.0 ** (-5.0 - jnp.arange(RET_HEADS, dtype=F32)))
    idx = jnp.arange(ln, dtype=F32)
    dist = idx[:, None] - idx[None, :]
    decay = jnp.where(dist >= 0, jnp.exp(jnp.maximum(dist, 0.0)[None] * log_g[:, None, None]), 0.0)
    q_decay = jnp.exp((idx + 1.0)[:, None] * log_g[None, :])
    k_decay = jnp.exp((ln - 1.0 - idx)[:, None] * log_g[None, :])
    s_decay = jnp.exp(ln * log_g)[None, :]
    return (decay, jnp.repeat(q_decay, RET_VD, axis=1), jnp.repeat(k_decay, RET_QK, axis=1),
            jnp.repeat(s_decay, RET_VD, axis=1))


def _merge_ffn_kernel(x_ref, gm_ref, oa_ref, or_ref, ga_ref, gr_ref, wa_ref, wr_ref, wo_ref,
                      sh_ref, sc_ref, g_ref, nw_ref, wup_ref, wdn_ref, nf_ref, o_ref):
    x = x_ref[...]
    nb, ln, _ = x.shape

    def flat(ref):
        return ref[...].reshape(nb * ln, ref.shape[-1])

    a = _dot(flat(oa_ref), wa_ref[...])
    r = _dot(flat(or_ref), wr_ref[...])
    merged = jax.nn.sigmoid(flat(ga_ref).astype(F32)) * a + jax.nn.sigmoid(flat(gr_ref).astype(F32)) * r
    y = _dot(merged.astype(BF16), wo_ref[...]).reshape(nb, ln, D_MODEL)
    x = x + gm_ref[...] * y
    x = _half_step(x, sh_ref, sc_ref, g_ref, nw_ref, wup_ref, wdn_ref)
    o_ref[...] = _rms(x, nf_ref[...])


def _merge_ffn(x, gate_m, o_a, o_r, gate_a, gate_r, w_a, w_r, w_o, shift, scale, gate, norm_w, w_up, w_down,
               norm_final):
    nb, ln, grid = _token_grid(x)
    return pl.pallas_call(
        _merge_ffn_kernel,
        grid=grid,
        in_specs=[_tok_spec(nb, ln, D_MODEL), _seq_spec(nb), _tok_spec(nb, ln, DA_W), _tok_spec(nb, ln, RET_V_W),
                  _tok_spec(nb, ln, D_MODEL), _tok_spec(nb, ln, D_MODEL),
                  _const_spec(w_a.shape), _const_spec(w_r.shape), _const_spec(w_o.shape),
                  _seq_spec(nb), _seq_spec(nb), _seq_spec(nb),
                  _const_spec((1, D_MODEL)), _const_spec(w_up.shape), _const_spec(w_down.shape),
                  _const_spec((1, D_MODEL))],
        out_specs=_tok_spec(nb, ln, D_MODEL),
        out_shape=jax.ShapeDtypeStruct(x.shape, F32),
        compiler_params=_params("parallel", "parallel"),
        name="merge_ffn",
    )(x, gate_m, o_a, o_r, gate_a, gate_r, w_a, w_r, w_o, shift, scale, gate, norm_w, w_up, w_down, norm_final)


def _before_attention(x, mod, w, state, offset):
    sh1, sc1, g1, shm, scm = [mod[:, i:i + 1] for i in range(5)]
    x = _ffn(x, sh1, sc1, g1, w["norm_f1"], w["w_up1"], w["w_down1"])
    pos = offset + jnp.arange(x.shape[1])
    return x, _proj(x, shm, scm, w["norm_mix"], w["w_in"], pos, w["ret_norm"], state, transposed=state is None)


def _after_attention(x, mod, w, norm_final, o_a, o_r, gate_a, gate_r):
    gm, sh2, sc2, g2 = [mod[:, i:i + 1] for i in range(5, N_MOD)]
    return _merge_ffn(x, gm, o_a, o_r, gate_a, gate_r, w["w_a_proj"], w["w_r_proj"], w["w_o"],
                      sh2, sc2, g2, w["norm_f2"], w["w_up2"], w["w_down2"], norm_final)


def kernel(x_prompt, x_sample, c_prompt, c_sample, cache_k, cache_v, state_ret, norm_f1, w_up1, w_down1, norm_mix, w_in, lambda_q1, lambda_k1, lambda_q2, lambda_k2, da_norm, ret_norm, w_a_proj, w_r_proj, w_o, norm_f2, w_up2, w_down2, w_ada, b_ada, norm_final):
    assert cache_k.shape[0] == 1, "single layer"
    n_prompt = c_prompt.shape[0]
    mod = _modulation(jnp.concatenate([c_prompt, c_sample], axis=0), w_ada[0], b_ada[0])
    mod = mod.reshape(-1, N_MOD, D_MODEL)
    w = {
        "norm_f1": norm_f1, "norm_mix": norm_mix, "norm_f2": norm_f2,
        "w_up1": w_up1[0].astype(BF16), "w_down1": w_down1[0].astype(BF16),
        "w_up2": w_up2[0].astype(BF16), "w_down2": w_down2[0].astype(BF16),
        "w_in": w_in[0].astype(BF16), "w_a_proj": w_a_proj[0].astype(BF16),
        "w_r_proj": w_r_proj[0].astype(BF16), "w_o": w_o[0].astype(BF16),
        "lambdas": (lambda_q1, lambda_k1, lambda_q2, lambda_k2),
        "ret_norm": ret_norm[0].reshape(1, RET_V_W),
    }
    nf = norm_final.reshape(1, D_MODEL)
    dec_b, past_len = cache_k.shape[1], cache_k.shape[2]
    cache_kt = jnp.transpose(cache_k[0], (0, 2, 3, 4, 1)).reshape(dec_b, DA_HEADS, HEAD_W, past_len)
    cache_v4 = cache_v[0].reshape(dec_b, past_len * DA_HEADS, HEAD_W)
    mod_p, mod_s = mod[:n_prompt], mod[n_prompt:]
    bp, seq = x_prompt.shape[:2]
    ln = x_sample.shape[1]

    x_s, (q_s, k_s, kb_s, v_s, vb_s, or_s, ga_s, gr_s, st_s) = _before_attention(
        x_sample, mod_s, w, state_ret[0], past_len)
    x_p, (qt_p, kb_p, kt_p, vt_p, v4_p, or_p, ga_p, gr_p, st_p) = _before_attention(x_prompt, mod_p, w, None, 0)
    oa_p, oa_s = _attention(qt_p, kb_p, vt_p, q_s, kb_s, vb_s, cache_kt, cache_v4, w["lambdas"],
                            da_norm[0].reshape(DA_HEADS, 1, HEAD_W))
    y_p = _after_attention(x_p, mod_p, w, nf, oa_p, or_p, ga_p, gr_p)
    y_s = _after_attention(x_s, mod_s, w, nf, oa_s, or_s, ga_s, gr_s)

    k_new_p = jnp.transpose(kt_p.reshape(bp, DA_HEADS, 2, DA_HD, seq), (0, 4, 1, 2, 3))[None]
    v_new_p = v4_p.reshape(1, bp, seq, DA_HEADS, HEAD_W)
    k_new_s = k_s.reshape(1, dec_b, ln, DA_HEADS, 2, DA_HD)
    v_new_s = v_s.reshape(1, dec_b, ln, DA_HEADS, HEAD_W)
    return y_p, y_s, k_new_p, v_new_p, st_p[None], k_new_s, v_new_s, st_s[None]
```

```python
import functools
import math

import jax
import jax.numpy as jnp
from jax import lax
from jax.experimental import pallas as pl
from jax.experimental.pallas import tpu as pltpu

F32 = jnp.float32
BF16 = jnp.bfloat16

D_MODEL = 1024
D_FF = 2816
N_MOD = 9
CHUNK = 64
DA_HEADS = 4
DA_HD = 64
DA_W = 512
HEAD_W = 128
RET_HEADS = 4
RET_QK = 64
RET_QK_W = 256
RET_VD = 128
RET_V_W = 512
ROPE_BASE = 10000.0
EPS = 1e-6
NEG_INF = -1e30
LAM_INIT = 0.8 - 0.6 * math.exp(-0.3 * 0)
LOG2_E = math.log2(math.e)
ONES_ROWS = 16

_IN_SIZES = (DA_W, DA_W, DA_W, RET_QK_W, RET_QK_W, RET_V_W, RET_V_W, D_MODEL, D_MODEL)
_IN_OFF = [0]
for _s in _IN_SIZES:
    _IN_OFF.append(_IN_OFF[-1] + _s)
IN_WIDTH = _IN_OFF[-1]

TOKEN_TILE = 512
FFN_TOKEN_TILE = 1024
ATTN_T = 256
SAMPLE_HEADS_PER_STEP = 2
RET_BLOCK = 256
V7X_VMEM_BYTES = 64 * 1024 * 1024
VMEM_LIMIT = V7X_VMEM_BYTES * 7 // 8


def _const_spec(shape):
    nd = len(shape)
    return pl.BlockSpec(shape, lambda *_: (0,) * nd, pipeline_mode=pl.Buffered(1))


def _params(*sem):
    return pltpu.CompilerParams(dimension_semantics=sem, vmem_limit_bytes=VMEM_LIMIT)


def _rms(x, w):
    ms = jnp.mean(x * x, axis=-1, keepdims=True)
    return x * lax.rsqrt(ms + EPS) * w


def _silu(x):
    return x * jax.nn.sigmoid(x)


def _dot(a, b):
    return jnp.dot(a, b, preferred_element_type=F32)


def _dot_nt(a, b):
    return lax.dot_general(a, b, (((1,), (1,)), ((), ())), preferred_element_type=F32)


def _dot_tn(a, b):
    return lax.dot_general(a, b, (((0,), (0,)), ((), ())), preferred_element_type=F32)


def _mod_kernel(c_ref, w_ref, b_ref, o_ref):
    s = _silu(c_ref[...]).astype(BF16)
    o_ref[...] = _dot(s, w_ref[...].astype(BF16)) + b_ref[...]


def _modulation(c, w_ada, b_ada):
    n, width = c.shape[0], w_ada.shape[1]
    tn = width // 8
    return pl.pallas_call(
        _mod_kernel,
        grid=(width // tn,),
        in_specs=[_const_spec((n, D_MODEL)),
                  pl.BlockSpec((D_MODEL, tn), lambda j: (0, j)),
                  pl.BlockSpec((1, tn), lambda j: (0, j))],
        out_specs=pl.BlockSpec((n, tn), lambda j: (0, j)),
        out_shape=jax.ShapeDtypeStruct((n, width), F32),
        compiler_params=_params("parallel"),
        name="mod",
    )(c, w_ada, b_ada.reshape(1, width))


def _token_grid(x, tile=TOKEN_TILE):
    bsz, seq, _ = x.shape
    ln = min(seq, tile)
    nb = tile // ln
    assert seq % ln == 0 and bsz % nb == 0
    return nb, ln, (bsz // nb, seq // ln)


def _tok_spec(nb, ln, width):
    return pl.BlockSpec((nb, ln, width), lambda b, s: (b, s, 0))


def _seq_spec(nb):
    return pl.BlockSpec((nb, 1, D_MODEL), lambda b, s: (b, 0, 0))


def _half_step(x, sh_ref, sc_ref, g_ref, nw_ref, wup_ref, wdn_ref):
    nb, ln, _ = x.shape
    h = _rms(x, nw_ref[...]) * (1.0 + sc_ref[...]) + sh_ref[...]
    hb = h.reshape(nb * ln, D_MODEL).astype(BF16)
    a = _dot(hb, wup_ref[:, :D_FF])
    b = _dot(hb, wup_ref[:, D_FF:])
    act = (_silu(a) * b).astype(BF16)
    y = _dot(act, wdn_ref[...]).reshape(nb, ln, D_MODEL)
    return x + 0.5 * g_ref[...] * y


def _ffn_kernel(x_ref, sh_ref, sc_ref, g_ref, nw_ref, wup_ref, wdn_ref, o_ref):
    o_ref[...] = _half_step(x_ref[...], sh_ref, sc_ref, g_ref, nw_ref, wup_ref, wdn_ref)


def _ffn(x, shift, scale, gate, norm_w, w_up, w_down):
    nb, ln, grid = _token_grid(x, FFN_TOKEN_TILE)
    return pl.pallas_call(
        _ffn_kernel,
        grid=grid,
        in_specs=[_tok_spec(nb, ln, D_MODEL), _seq_spec(nb), _seq_spec(nb), _seq_spec(nb),
                  _const_spec((1, D_MODEL)), _const_spec(w_up.shape), _const_spec(w_down.shape)],
        out_specs=_tok_spec(nb, ln, D_MODEL),
        out_shape=jax.ShapeDtypeStruct(x.shape, F32),
        compiler_params=_params("parallel", "parallel"),
        name="ffn",
    )(x, shift, scale, gate, norm_w, w_up, w_down)


def _rotary(x, cos, sin_signed):
    lane = lax.broadcasted_iota(jnp.int32, x.shape, 1)
    first_half = (lane % RET_QK) < (RET_QK // 2)
    width = x.shape[1]
    partner = jnp.where(first_half, pltpu.roll(x, width - RET_QK // 2, 1), pltpu.roll(x, RET_QK // 2, 1))
    return x * cos + partner * sin_signed


def _proj_kernel(*refs, transposed, has_state):
    (x_ref, sh_ref, sc_ref, nw_ref, win_ref, cos_ref, sin_ref,
     dec_ref, qd_ref, kd_ref, sd_ref, rn_ref), refs = refs[:12], refs[12:]
    if has_state:
        s0_ref, refs = refs[0], refs[1:]
    a_refs, (or_ref, ga_ref, gr_ref, sout_ref, st_ref) = refs[:5], refs[5:]
    x = x_ref[...]
    nb, ln, _ = x.shape
    h = _rms(x, nw_ref[...]) * (1.0 + sc_ref[...]) + sh_ref[...]
    hb = h.reshape(nb * ln, D_MODEL).astype(BF16)

    def proj(i):
        return _dot(hb, win_ref[:, _IN_OFF[i]:_IN_OFF[i + 1]])

    def put(ref, val):
        ref[...] = val.reshape(nb, ln, val.shape[-1]).astype(ref.dtype)

    q = proj(0) * (DA_HD ** -0.5 * LOG2_E)
    k = proj(1)
    v = proj(2)
    if transposed:
        qt_ref, kb_ref, kt_ref, vt_ref, v4_ref = a_refs
        qt_ref[0] = q.T.astype(BF16)
        kt_ref[0] = k.T
        vt_ref[0] = v.T.astype(BF16)
        for hd in range(DA_HEADS):
            kb_ref[0, hd] = k[:, hd * HEAD_W:(hd + 1) * HEAD_W].astype(BF16)
            v4_ref[0, pl.ds(hd, ln, stride=DA_HEADS), :] = v[:, hd * HEAD_W:(hd + 1) * HEAD_W]
    else:
        q_ref, k_ref, kb_ref, v_ref, vb_ref = a_refs
        put(q_ref, q)
        put(k_ref, k)
        put(kb_ref, k)
        put(v_ref, v)
        put(vb_ref, v)
    put(ga_ref, proj(7))
    put(gr_ref, proj(8))

    seq_step = pl.program_id(1)

    @pl.when(seq_step == 0)
    def _():
        st_ref[...] = jnp.zeros(st_ref.shape, F32)
        if has_state:
            for n in range(nb):
                for hd in range(RET_HEADS):
                    st_ref[n, hd * RET_QK:(hd + 1) * RET_QK, hd * RET_VD:(hd + 1) * RET_VD] = s0_ref[n, hd]

    cos, sin_signed = cos_ref[...], sin_ref[...]
    q_r = _rotary(proj(3), cos, sin_signed).astype(BF16)
    k_r = _rotary(proj(4), cos, sin_signed) * (RET_QK ** -0.5)
    v_r = proj(5).astype(BF16)
    z_r = proj(6)
    rl = dec_ref.shape[-1]
    for n in range(nb):
        for blk in range(ln // rl):
            rows = slice(n * ln + blk * rl, n * ln + (blk + 1) * rl)
            o, st_ref[n] = _retention_block(q_r[rows], k_r[rows], v_r[rows], st_ref[n],
                                            dec_ref, qd_ref, kd_ref, sd_ref)
            o = o * rn_ref[...] * _silu(z_r[rows])
            or_ref[n, blk * rl:(blk + 1) * rl, :] = o.astype(or_ref.dtype)

    @pl.when(seq_step == pl.num_programs(1) - 1)
    def _():
        for n in range(nb):
            for hd in range(RET_HEADS):
                sout_ref[n, hd] = st_ref[n, hd * RET_QK:(hd + 1) * RET_QK, hd * RET_VD:(hd + 1) * RET_VD]


def _rope_tables(pos, nb):
    half = RET_QK // 2
    inv = ROPE_BASE ** (-jnp.arange(half, dtype=F32) / half)
    ang = pos.astype(F32)[:, None] * inv[None, :]
    cos, sin = jnp.cos(ang), jnp.sin(ang)
    cos = jnp.tile(jnp.concatenate([cos, cos], axis=-1), (nb, RET_HEADS))
    sin = jnp.tile(jnp.concatenate([-sin, sin], axis=-1), (nb, RET_HEADS))
    return cos, sin


def _proj(x, shift, scale, norm_w, w_in, pos, ret_norm, state, transposed):
    bsz, seq, _ = x.shape
    nb, ln, grid = _token_grid(x)
    cos, sin = _rope_tables(pos, nb)
    tab_spec = pl.BlockSpec((nb * ln, RET_QK_W), lambda b, s: (s, 0))
    rl = min(ln, RET_BLOCK)
    assert ln % rl == 0
    tables = _retention_tables(rl)
    has_state = state is not None
    state_spec = pl.BlockSpec((nb, RET_HEADS, RET_QK, RET_VD), lambda b, s: (b, 0, 0, 0))
    state_shape = jax.ShapeDtypeStruct((bsz, RET_HEADS, RET_QK, RET_VD), F32)
    if transposed:
        assert nb == 1
        t_spec = pl.BlockSpec((1, DA_W, ln), lambda b, s: (b, 0, s))
        t_shape = (bsz, DA_W, seq)
        a_specs = [t_spec, pl.BlockSpec((1, DA_HEADS, ln, HEAD_W), lambda b, s: (b, 0, s, 0)), t_spec, t_spec,
                   pl.BlockSpec((1, DA_HEADS * ln, HEAD_W), lambda b, s: (b, s, 0))]
        a_shapes = [jax.ShapeDtypeStruct(t_shape, BF16), jax.ShapeDtypeStruct((bsz, DA_HEADS, seq, HEAD_W), BF16),
                    jax.ShapeDtypeStruct(t_shape, F32), jax.ShapeDtypeStruct(t_shape, BF16),
                    jax.ShapeDtypeStruct((bsz, DA_HEADS * seq, HEAD_W), F32)]
    else:
        a_specs = [_tok_spec(nb, ln, DA_W)] * 5
        a_shapes = [jax.ShapeDtypeStruct((bsz, seq, DA_W), dt) for dt in (BF16, F32, BF16, F32, BF16)]
    widths = (RET_V_W, D_MODEL, D_MODEL)
    in_specs = [_tok_spec(nb, ln, D_MODEL), _seq_spec(nb), _seq_spec(nb),
                _const_spec((1, D_MODEL)), _const_spec(w_in.shape), tab_spec, tab_spec]
    in_specs += [_const_spec(tab.shape) for tab in tables] + [_const_spec((1, RET_V_W))]
    args = [x, shift, scale, norm_w, w_in, cos, sin, *tables, ret_norm]
    if has_state:
        in_specs.append(state_spec)
        args.append(state)
    return pl.pallas_call(
        functools.partial(_proj_kernel, transposed=transposed, has_state=has_state),
        grid=grid,
        in_specs=in_specs,
        out_specs=a_specs + [_tok_spec(nb, ln, w) for w in widths] + [state_spec],
        out_shape=a_shapes + [jax.ShapeDtypeStruct((bsz, seq, w), BF16) for w in widths] + [state_shape],
        scratch_shapes=[pltpu.VMEM((nb, RET_QK_W, RET_V_W), F32)],
        compiler_params=_params("parallel", "arbitrary"),
        name="proj_t" if transposed else "proj",
    )(*args)


def _lambda(lq1_ref, lk1_ref, lq2_ref, lk2_ref):
    s1 = jnp.sum(lq1_ref[...] * lk1_ref[...], axis=-1, keepdims=True)
    s2 = jnp.sum(lq2_ref[...] * lk2_ref[...], axis=-1, keepdims=True)
    return jnp.exp(s1) - jnp.exp(s2) + LAM_INIT


def _stack_maps(q):
    lane = lax.broadcasted_iota(jnp.int32, q.shape, 1)
    zero = jnp.zeros_like(q)
    return jnp.concatenate([jnp.where(lane < DA_HD, q, zero), jnp.where(lane >= DA_HD, q, zero)], axis=0)


def _prompt_attention(lam, nw_ref, qt_ref, k_ref, vt_ref, o_ref):
    heads, seq = k_ref.shape[1], k_ref.shape[2]
    t = ATTN_T
    feat = lax.broadcasted_iota(jnp.int32, (HEAD_W, t), 0)
    key_chunk = lax.broadcasted_iota(jnp.int32, (t, t), 0) // CHUNK
    query_chunk = lax.broadcasted_iota(jnp.int32, (t, t), 1) // CHUNK
    visible = key_chunk <= query_chunk
    visible2 = jnp.concatenate([visible, visible], axis=1)

    def head_cols(h):
        return slice(h * HEAD_W, (h + 1) * HEAD_W)

    def query_maps(i, h):
        qt = qt_ref[0, head_cols(h), i * t:(i + 1) * t]
        zero = jnp.zeros_like(qt)
        return jnp.concatenate([jnp.where(feat < DA_HD, qt, zero), jnp.where(feat >= DA_HD, qt, zero)], axis=1)

    def scores(i, h):
        lo, hi = i * t, (i + 1) * t
        qm = query_maps(i, h)
        s_diag = jnp.where(visible2, _dot(k_ref[0, h, lo:hi, :], qm), NEG_INF)
        m = jnp.max(s_diag, axis=0, keepdims=True)
        s_full = None
        if i > 0:
            s_full = _dot(k_ref[0, h, :lo, :], qm)
            m = jnp.maximum(m, jnp.max(s_full, axis=0, keepdims=True))
        return s_diag, s_full, m

    ones = jnp.ones((ONES_ROWS, seq), BF16)
    vt1 = [jnp.concatenate([vt_ref[0, head_cols(h), :], ones], axis=0) for h in range(heads)]

    def finish(i, h, s_diag, s_full, m):
        lo, hi = i * t, (i + 1) * t
        acc = _dot(vt1[h][:, lo:hi], jnp.exp2(s_diag - m).astype(BF16))
        if i > 0:
            acc = acc + _dot(vt1[h][:, :lo], jnp.exp2(s_full - m).astype(BF16))
        o2 = acc[:HEAD_W] / acc[HEAD_W:HEAD_W + 1]
        ot = o2[:, :t] - lam * o2[:, t:]
        o_ref[0, h, lo:hi, :] = (_rms(ot.T, nw_ref[h]) * (1.0 - LAM_INIT)).astype(o_ref.dtype)

    n_blocks = seq // t
    pending = [scores(0, h) for h in range(heads)]
    yield
    for i in range(1, n_blocks):
        upcoming = [scores(i, h) for h in range(heads)]
        for h in range(heads):
            finish(i - 1, h, *pending[h])
        pending = upcoming
        yield
    for h in range(heads):
        finish(n_blocks - 1, h, *pending[h])


def _sample_attention(lam, nw_ref, heads, q_ref, kn_ref, vn_ref, kc_ref, vc_ref, o_ref):
    ln = q_ref.shape[1]
    past = kc_ref.shape[-1]
    ones = jnp.ones((past, HEAD_W), BF16)

    def head_cols(j):
        return slice(j * HEAD_W, (j + 1) * HEAD_W)

    def scores(j):
        qs = _stack_maps(q_ref[0, :, head_cols(j)])
        s_cache = _dot(qs, kc_ref[0, j].astype(BF16))
        s_new = _dot_nt(qs, kn_ref[0, :, head_cols(j)])
        m = jnp.maximum(jnp.max(s_cache, axis=-1, keepdims=True), jnp.max(s_new, axis=-1, keepdims=True))
        return s_cache, s_new, m

    def finish(j, s_cache, s_new, m):
        v_cache = vc_ref[0, pl.ds(heads[j], past, stride=DA_HEADS), :].astype(BF16)
        acc = _dot(jnp.exp2(s_cache - m).astype(BF16), jnp.concatenate([v_cache, ones], axis=1))
        v_new = jnp.concatenate([vn_ref[0, :, head_cols(j)], ones[:ln]], axis=1)
        acc = acc + _dot(jnp.exp2(s_new - m).astype(BF16), v_new)
        o2 = acc[:, :HEAD_W] / acc[:, HEAD_W:HEAD_W + 1]
        o = o2[:ln] - lam * o2[ln:]
        o_ref[0, j] = (_rms(o, nw_ref[heads[j]]) * (1.0 - LAM_INIT)).astype(o_ref.dtype)

    pending = scores(0)
    yield
    for j in range(1, len(heads)):
        upcoming = scores(j)
        yield
        finish(j - 1, *pending)
        yield
        pending = upcoming
    finish(len(heads) - 1, *pending)


def _weave(main, side, main_per_side=2):
    done = object()
    main_live = side_live = True
    while main_live or side_live:
        if side_live:
            side_live = next(side, done) is not done
        for _ in range(main_per_side):
            if main_live:
                main_live = next(main, done) is not done


def _attn_kernel(lq1_ref, lk1_ref, lq2_ref, lk2_ref, nwp_ref, nws_ref, qt_ref, k_ref, vt_ref,
                 q_ref, kn_ref, vn_ref, kc_ref, vc_ref, op_ref, os_ref):
    lam = _lambda(lq1_ref, lk1_ref, lq2_ref, lk2_ref)
    g = SAMPLE_HEADS_PER_STEP
    group = pl.program_id(0) % (DA_HEADS // g)
    for grp in range(DA_HEADS // g):
        @pl.when(group == grp)
        def _(grp=grp):
            heads = tuple(range(grp * g, (grp + 1) * g))
            _weave(_prompt_attention(lam, nwp_ref, qt_ref, k_ref, vt_ref, op_ref),
                   _sample_attention(lam, nws_ref, heads, q_ref, kn_ref, vn_ref, kc_ref, vc_ref, os_ref))


def _attention(qt, k, vt, q, k_new, v_new, cache_kt, cache_v4, lambdas, da_norm):
    bp, _, seq, _ = k.shape
    bs, ln, _ = q.shape
    past = cache_kt.shape[-1]
    g = SAMPLE_HEADS_PER_STEP
    groups = DA_HEADS // g
    steps = bp * DA_HEADS
    assert steps == bs * groups, "one prompt (sequence, head) per sample (sequence, head group)"
    assert seq % ATTN_T == 0 and ATTN_T % CHUNK == 0 and past % CHUNK == 0 and ln <= CHUNK
    lam_spec = _const_spec((1, DA_HD))
    t_spec = pl.BlockSpec((1, HEAD_W, seq), lambda i: (i // DA_HEADS, i % DA_HEADS, 0))
    head_spec = pl.BlockSpec((1, 1, seq, HEAD_W), lambda i: (i // DA_HEADS, i % DA_HEADS, 0, 0))
    new_spec = pl.BlockSpec((1, ln, g * HEAD_W), lambda i: (i // groups, 0, i % groups))
    out_new_spec = pl.BlockSpec((1, g, ln, HEAD_W), lambda i: (i // groups, i % groups, 0, 0))
    return pl.pallas_call(
        _attn_kernel,
        grid=(steps,),
        in_specs=[lam_spec] * 4 + [pl.BlockSpec((1, 1, HEAD_W), lambda i: (i % DA_HEADS, 0, 0)),
                                   _const_spec((DA_HEADS, 1, HEAD_W)), t_spec, head_spec, t_spec,
                                   new_spec, new_spec, new_spec,
                                   pl.BlockSpec((1, g, HEAD_W, past), lambda i: (i // groups, i % groups, 0, 0)),
                                   pl.BlockSpec((1, DA_HEADS * past, HEAD_W), lambda i: (i // groups, 0, 0))],
        out_specs=[head_spec, out_new_spec],
        out_shape=[jax.ShapeDtypeStruct((bp, DA_HEADS, seq, HEAD_W), BF16),
                   jax.ShapeDtypeStruct((bs, DA_HEADS, ln, HEAD_W), BF16)],
        compiler_params=_params("parallel"),
        name="attn",
    )(*lambdas, da_norm, da_norm, qt, k, vt, q, k_new, v_new, cache_kt, cache_v4)


def _retention_block(q, k_f32, v, state, dec_ref, qd_ref, kd_ref, sd_ref):
    k = k_f32.astype(BF16)
    cross = _dot(q, state.astype(BF16)) * qd_ref[...]
    lane = lax.broadcasted_iota(jnp.int32, q.shape, 1)
    outs = []
    for h in range(RET_HEADS):
        qh = jnp.where((lane >= h * RET_QK) & (lane < (h + 1) * RET_QK), q, jnp.zeros_like(q))
        scores = _dot_nt(qh, k) * dec_ref[h]
        cols = slice(h * RET_VD, (h + 1) * RET_VD)
        oh = _dot(scores.astype(BF16), v[:, cols]) + cross[:, cols]
        xc = oh - jnp.mean(oh, axis=-1, keepdims=True)
        outs.append(xc * lax.rsqrt(jnp.mean(xc * xc, axis=-1, keepdims=True) + EPS))
    kd = (k_f32 * kd_ref[...]).astype(BF16)
    upd = _dot_tn(kd, v)
    r = lax.broadcasted_iota(jnp.int32, upd.shape, 0) // RET_QK
    cl = lax.broadcasted_iota(jnp.int32, upd.shape, 1) // RET_VD
    new_state = state * sd_ref[...] + jnp.where(r == cl, upd, 0.0)
    return jnp.concatenate(outs, axis=-1), new_state


def _retention_tables(ln):
    log_g = jnp.log(1.0 - 2.0 ** (-5.0 - jnp.arange(RET_HEADS, dtype=F32)))
    idx = jnp.arange(ln, dtype=F32)
    dist = idx[:, None] - idx[None, :]
    decay = jnp.where(dist >= 0, jnp.exp(jnp.maximum(dist, 0.0)[None] * log_g[:, None, None]), 0.0)
    q_decay = jnp.exp((idx + 1.0)[:, None] * log_g[None, :])
    k_decay = jnp.exp((ln - 1.0 - idx)[:, None] * log_g[None, :])
    s_decay = jnp.exp(ln * log_g)[None, :]
    return (decay, jnp.repeat(q_decay, RET_VD, axis=1), jnp.repeat(k_decay, RET_QK, axis=1),
            jnp.repeat(s_decay, RET_VD, axis=1))


def _merge_ffn_kernel(x_ref, gm_ref, oa_ref, or_ref, ga_ref, gr_ref, wa_ref, wr_ref, wo_ref,
                      sh_ref, sc_ref, g_ref, nw_ref, wup_ref, wdn_ref, nf_ref, o_ref):
    x = x_ref[...]
    nb, ln, _ = x.shape

    def flat(ref):
        return ref[...].reshape(nb * ln, ref.shape[-1])

    o_a = jnp.concatenate([oa_ref[:, hd] for hd in range(DA_HEADS)], axis=-1)
    a = _dot(o_a.reshape(nb * ln, DA_W), wa_ref[...])
    r = _dot(flat(or_ref), wr_ref[...])
    merged = jax.nn.sigmoid(flat(ga_ref).astype(F32)) * a + jax.nn.sigmoid(flat(gr_ref).astype(F32)) * r
    y = _dot(merged.astype(BF16), wo_ref[...]).reshape(nb, ln, D_MODEL)
    x = x + gm_ref[...] * y
    x = _half_step(x, sh_ref, sc_ref, g_ref, nw_ref, wup_ref, wdn_ref)
    o_ref[...] = _rms(x, nf_ref[...])


def _merge_ffn(x, gate_m, o_a, o_r, gate_a, gate_r, w_a, w_r, w_o, shift, scale, gate, norm_w, w_up, w_down,
               norm_final):
    nb, ln, grid = _token_grid(x)
    return pl.pallas_call(
        _merge_ffn_kernel,
        grid=grid,
        in_specs=[_tok_spec(nb, ln, D_MODEL), _seq_spec(nb),
                  pl.BlockSpec((nb, DA_HEADS, ln, HEAD_W), lambda b, s: (b, 0, s, 0)), _tok_spec(nb, ln, RET_V_W),
                  _tok_spec(nb, ln, D_MODEL), _tok_spec(nb, ln, D_MODEL),
                  _const_spec(w_a.shape), _const_spec(w_r.shape), _const_spec(w_o.shape),
                  _seq_spec(nb), _seq_spec(nb), _seq_spec(nb),
                  _const_spec((1, D_MODEL)), _const_spec(w_up.shape), _const_spec(w_down.shape),
                  _const_spec((1, D_MODEL))],
        out_specs=_tok_spec(nb, ln, D_MODEL),
        out_shape=jax.ShapeDtypeStruct(x.shape, F32),
        compiler_params=_params("parallel", "parallel"),
        name="merge_ffn",
    )(x, gate_m, o_a, o_r, gate_a, gate_r, w_a, w_r, w_o, shift, scale, gate, norm_w, w_up, w_down, norm_final)


def _before_attention(x, mod, w, state, offset):
    sh1, sc1, g1, shm, scm = [mod[:, i:i + 1] for i in range(5)]
    x = _ffn(x, sh1, sc1, g1, w["norm_f1"], w["w_up1"], w["w_down1"])
    pos = offset + jnp.arange(x.shape[1])
    return x, _proj(x, shm, scm, w["norm_mix"], w["w_in"], pos, w["ret_norm"], state, transposed=state is None)


def _after_attention(x, mod, w, norm_final, o_a, o_r, gate_a, gate_r):
    gm, sh2, sc2, g2 = [mod[:, i:i + 1] for i in range(5, N_MOD)]
    return _merge_ffn(x, gm, o_a, o_r, gate_a, gate_r, w["w_a_proj"], w["w_r_proj"], w["w_o"],
                      sh2, sc2, g2, w["norm_f2"], w["w_up2"], w["w_down2"], norm_final)


def kernel(x_prompt, x_sample, c_prompt, c_sample, cache_k, cache_v, state_ret, norm_f1, w_up1, w_down1, norm_mix, w_in, lambda_q1, lambda_k1, lambda_q2, lambda_k2, da_norm, ret_norm, w_a_proj, w_r_proj, w_o, norm_f2, w_up2, w_down2, w_ada, b_ada, norm_final):
    assert cache_k.shape[0] == 1, "single layer"
    n_prompt = c_prompt.shape[0]
    mod = _modulation(jnp.concatenate([c_prompt, c_sample], axis=0), w_ada[0], b_ada[0])
    mod = mod.reshape(-1, N_MOD, D_MODEL)
    w = {
        "norm_f1": norm_f1, "norm_mix": norm_mix, "norm_f2": norm_f2,
        "w_up1": w_up1[0].astype(BF16), "w_down1": w_down1[0].astype(BF16),
        "w_up2": w_up2[0].astype(BF16), "w_down2": w_down2[0].astype(BF16),
        "w_in": w_in[0].astype(BF16), "w_a_proj": w_a_proj[0].astype(BF16),
        "w_r_proj": w_r_proj[0].astype(BF16), "w_o": w_o[0].astype(BF16),
        "lambdas": (lambda_q1, lambda_k1, lambda_q2, lambda_k2),
        "ret_norm": ret_norm[0].reshape(1, RET_V_W),
    }
    nf = norm_final.reshape(1, D_MODEL)
    dec_b, past_len = cache_k.shape[1], cache_k.shape[2]
    cache_kt = jnp.transpose(cache_k[0], (0, 2, 3, 4, 1)).reshape(dec_b, DA_HEADS, HEAD_W, past_len)
    cache_v4 = cache_v[0].reshape(dec_b, past_len * DA_HEADS, HEAD_W)
    mod_p, mod_s = mod[:n_prompt], mod[n_prompt:]
    bp, seq = x_prompt.shape[:2]
    ln = x_sample.shape[1]

    x_s, (q_s, k_s, kb_s, v_s, vb_s, or_s, ga_s, gr_s, st_s) = _before_attention(
        x_sample, mod_s, w, state_ret[0], past_len)
    x_p, (qt_p, kb_p, kt_p, vt_p, v4_p, or_p, ga_p, gr_p, st_p) = _before_attention(x_prompt, mod_p, w, None, 0)
    oa_p, oa_s = _attention(qt_p, kb_p, vt_p, q_s, kb_s, vb_s, cache_kt, cache_v4, w["lambdas"],
                            da_norm[0].reshape(DA_HEADS, 1, HEAD_W))
    y_p = _after_attention(x_p, mod_p, w, nf, oa_p, or_p, ga_p, gr_p)
    y_s = _after_attention(x_s, mod_s, w, nf, oa_s, or_s, ga_s, gr_s)

    k_new_p = jnp.transpose(kt_p.reshape(bp, DA_HEADS, 2, DA_HD, seq), (0, 4, 1, 2, 3))[None]
    v_new_p = v4_p.reshape(1, bp, seq, DA_HEADS, HEAD_W)
    k_new_s = k_s.reshape(1, dec_b, ln, DA_HEADS, 2, DA_HD)
    v_new_s = v_s.reshape(1, dec_b, ln, DA_HEADS, HEAD_W)
    return y_p, y_s, k_new_p, v_new_p, st_p[None], k_new_s, v_new_s, st_s[None]
```

```python
import functools
import math

import jax
import jax.numpy as jnp
import numpy as np
from jax import lax
from jax.experimental import pallas as pl
from jax.experimental.pallas import tpu as pltpu

F32 = jnp.float32
BF16 = jnp.bfloat16

D_MODEL = 1024
D_FF = 2816
N_MOD = 9
CHUNK = 64
DA_HEADS = 4
DA_HD = 64
DA_W = 512
HEAD_W = 128
RET_HEADS = 4
RET_QK = 64
RET_QK_W = 256
RET_VD = 128
RET_V_W = 512
ROPE_BASE = 10000.0
EPS = 1e-6
NEG_INF = -1e30
LAM_INIT = 0.8 - 0.6 * math.exp(-0.3 * 0)
LOG2_E = math.log2(math.e)
ONES_ROWS = 16

_IN_SIZES = (DA_W, DA_W, DA_W, RET_QK_W, RET_QK_W, RET_V_W, RET_V_W, D_MODEL, D_MODEL)
_IN_OFF = [0]
for _s in _IN_SIZES:
    _IN_OFF.append(_IN_OFF[-1] + _s)
IN_WIDTH = _IN_OFF[-1]

TOKEN_TILE = 512
FFN_TOKEN_TILE = 1024
ATTN_T = 256
SAMPLE_HEADS_PER_STEP = 2
RET_BLOCK = 256
V7X_VMEM_BYTES = 64 * 1024 * 1024
VMEM_LIMIT = V7X_VMEM_BYTES * 7 // 8


def _const_spec(shape):
    nd = len(shape)
    return pl.BlockSpec(shape, lambda *_: (0,) * nd, pipeline_mode=pl.Buffered(1))


def _params(*sem):
    return pltpu.CompilerParams(dimension_semantics=sem, vmem_limit_bytes=VMEM_LIMIT)


def _rms(x, w):
    ms = jnp.mean(x * x, axis=-1, keepdims=True)
    return x * lax.rsqrt(ms + EPS) * w


def _silu(x):
    return x * jax.nn.sigmoid(x)


def _dot(a, b):
    return jnp.dot(a, b, preferred_element_type=F32)


def _dot_nt(a, b):
    return lax.dot_general(a, b, (((1,), (1,)), ((), ())), preferred_element_type=F32)


def _dot_tn(a, b):
    return lax.dot_general(a, b, (((0,), (0,)), ((), ())), preferred_element_type=F32)


def _mod_kernel(c_ref, w_ref, b_ref, o_ref):
    s = _silu(c_ref[...]).astype(BF16)
    o_ref[...] = _dot(s, w_ref[...].astype(BF16)) + b_ref[...]


def _modulation(c, w_ada, b_ada):
    n, width = c.shape[0], w_ada.shape[1]
    tn = width // 8
    return pl.pallas_call(
        _mod_kernel,
        grid=(width // tn,),
        in_specs=[_const_spec((n, D_MODEL)),
                  pl.BlockSpec((D_MODEL, tn), lambda j: (0, j)),
                  pl.BlockSpec((1, tn), lambda j: (0, j))],
        out_specs=pl.BlockSpec((n, tn), lambda j: (0, j)),
        out_shape=jax.ShapeDtypeStruct((n, width), F32),
        compiler_params=_params("parallel"),
        name="mod",
    )(c, w_ada, b_ada.reshape(1, width))


def _token_grid(x, tile=TOKEN_TILE):
    bsz, seq, _ = x.shape
    ln = min(seq, tile)
    nb = tile // ln
    assert seq % ln == 0 and bsz % nb == 0
    return nb, ln, (bsz // nb, seq // ln)


def _tok_spec(nb, ln, width):
    return pl.BlockSpec((nb, ln, width), lambda b, s: (b, s, 0))


def _seq_spec(nb):
    return pl.BlockSpec((nb, 1, D_MODEL), lambda b, s: (b, 0, 0))


def _half_step(x, sh_ref, sc_ref, g_ref, nw_ref, wup_ref, wdn_ref):
    nb, ln, _ = x.shape
    h = _rms(x, nw_ref[...]) * (1.0 + sc_ref[...]) + sh_ref[...]
    hb = h.reshape(nb * ln, D_MODEL).astype(BF16)
    a = _dot(hb, wup_ref[:, :D_FF])
    b = _dot(hb, wup_ref[:, D_FF:])
    act = (_silu(a) * b).astype(BF16)
    y = _dot(act, wdn_ref[...]).reshape(nb, ln, D_MODEL)
    return x + 0.5 * g_ref[...] * y


def _ffn_kernel(x_ref, sh_ref, sc_ref, g_ref, nw_ref, wup_ref, wdn_ref, o_ref):
    o_ref[...] = _half_step(x_ref[...], sh_ref, sc_ref, g_ref, nw_ref, wup_ref, wdn_ref)


def _ffn(x, shift, scale, gate, norm_w, w_up, w_down):
    nb, ln, grid = _token_grid(x, FFN_TOKEN_TILE)
    return pl.pallas_call(
        _ffn_kernel,
        grid=grid,
        in_specs=[_tok_spec(nb, ln, D_MODEL), _seq_spec(nb), _seq_spec(nb), _seq_spec(nb),
                  _const_spec((1, D_MODEL)), _const_spec(w_up.shape), _const_spec(w_down.shape)],
        out_specs=_tok_spec(nb, ln, D_MODEL),
        out_shape=jax.ShapeDtypeStruct(x.shape, F32),
        compiler_params=_params("parallel", "parallel"),
        name="ffn",
    )(x, shift, scale, gate, norm_w, w_up, w_down)


def _rotary(x, cos, sin_signed):
    lane = lax.broadcasted_iota(jnp.int32, x.shape, 1)
    first_half = (lane % RET_QK) < (RET_QK // 2)
    width = x.shape[1]
    partner = jnp.where(first_half, pltpu.roll(x, width - RET_QK // 2, 1), pltpu.roll(x, RET_QK // 2, 1))
    return x * cos + partner * sin_signed


def _proj_kernel(*refs, transposed, has_state):
    (x_ref, sh_ref, sc_ref, nw_ref, win_ref, cos_ref, sin_ref,
     dec_ref, qd_ref, kd_ref, sd_ref, rn_ref), refs = refs[:12], refs[12:]
    if has_state:
        s0_ref, refs = refs[0], refs[1:]
    a_refs, (or_ref, ga_ref, gr_ref, sout_ref, st_ref) = refs[:5], refs[5:]
    x = x_ref[...]
    nb, ln, _ = x.shape
    h = _rms(x, nw_ref[...]) * (1.0 + sc_ref[...]) + sh_ref[...]
    hb = h.reshape(nb * ln, D_MODEL).astype(BF16)

    def proj(i):
        return _dot(hb, win_ref[:, _IN_OFF[i]:_IN_OFF[i + 1]])

    def put(ref, val):
        ref[...] = val.reshape(nb, ln, val.shape[-1]).astype(ref.dtype)

    q = proj(0) * (DA_HD ** -0.5 * LOG2_E)
    k = proj(1)
    v = proj(2)
    if transposed:
        qt_ref, kb_ref, kt_ref, vt_ref, v4_ref = a_refs
        qt_ref[0] = q.T.astype(BF16)
        kt_ref[0] = k.T
        vt_ref[0] = v.T.astype(BF16)
        for hd in range(DA_HEADS):
            kb_ref[0, hd] = k[:, hd * HEAD_W:(hd + 1) * HEAD_W].astype(BF16)
            v4_ref[0, pl.ds(hd, ln, stride=DA_HEADS), :] = v[:, hd * HEAD_W:(hd + 1) * HEAD_W]
    else:
        q_ref, k_ref, kb_ref, v_ref, vb_ref = a_refs
        put(q_ref, q)
        put(k_ref, k)
        put(kb_ref, k)
        for n in range(nb):
            for hd in range(DA_HEADS):
                v_ref[n, pl.ds(hd, ln, stride=DA_HEADS), :] = v[n * ln:(n + 1) * ln, hd * HEAD_W:(hd + 1) * HEAD_W]
        put(vb_ref, v)
    put(ga_ref, proj(7))
    put(gr_ref, proj(8))

    seq_step = pl.program_id(1)

    @pl.when(seq_step == 0)
    def _():
        st_ref[...] = jnp.zeros(st_ref.shape, F32)
        if has_state:
            for n in range(nb):
                for hd in range(RET_HEADS):
                    st_ref[n, hd * RET_QK:(hd + 1) * RET_QK, hd * RET_VD:(hd + 1) * RET_VD] = s0_ref[n, hd]

    cos, sin_signed = cos_ref[...], sin_ref[...]
    q_r = _rotary(proj(3), cos, sin_signed).astype(BF16)
    k_r = _rotary(proj(4), cos, sin_signed) * (RET_QK ** -0.5)
    v_r = proj(5).astype(BF16)
    z_r = proj(6)
    rl = dec_ref.shape[-1]
    for n in range(nb):
        for blk in range(ln // rl):
            rows = slice(n * ln + blk * rl, n * ln + (blk + 1) * rl)
            o, st_ref[n] = _retention_block(q_r[rows], k_r[rows], v_r[rows], st_ref[n],
                                            dec_ref, qd_ref, kd_ref, sd_ref)
            o = o * rn_ref[...] * _silu(z_r[rows])
            or_ref[n, blk * rl:(blk + 1) * rl, :] = o.astype(or_ref.dtype)

    @pl.when(seq_step == pl.num_programs(1) - 1)
    def _():
        for n in range(nb):
            for hd in range(RET_HEADS):
                sout_ref[n, hd] = st_ref[n, hd * RET_QK:(hd + 1) * RET_QK, hd * RET_VD:(hd + 1) * RET_VD]


def _rope_tables(pos, nb):
    half = RET_QK // 2
    inv = ROPE_BASE ** (-np.arange(half, dtype=np.float64) / half)
    ang = np.asarray(pos, np.float64)[:, None] * inv[None, :]
    cos, sin = np.cos(ang), np.sin(ang)
    cos = np.tile(np.concatenate([cos, cos], axis=-1), (nb, RET_HEADS))
    sin = np.tile(np.concatenate([-sin, sin], axis=-1), (nb, RET_HEADS))
    return jnp.asarray(cos, F32), jnp.asarray(sin, F32)


def _proj(x, shift, scale, norm_w, w_in, pos, ret_norm, state, transposed):
    bsz, seq, _ = x.shape
    nb, ln, grid = _token_grid(x)
    cos, sin = _rope_tables(pos, nb)
    tab_spec = pl.BlockSpec((nb * ln, RET_QK_W), lambda b, s: (s, 0))
    rl = min(ln, RET_BLOCK)
    assert ln % rl == 0
    tables = _retention_tables(rl)
    has_state = state is not None
    state_spec = pl.BlockSpec((nb, RET_HEADS, RET_QK, RET_VD), lambda b, s: (b, 0, 0, 0))
    state_shape = jax.ShapeDtypeStruct((bsz, RET_HEADS, RET_QK, RET_VD), F32)
    if transposed:
        assert nb == 1
        t_spec = pl.BlockSpec((1, DA_W, ln), lambda b, s: (b, 0, s))
        t_shape = (bsz, DA_W, seq)
        a_specs = [t_spec, pl.BlockSpec((1, DA_HEADS, ln, HEAD_W), lambda b, s: (b, 0, s, 0)), t_spec, t_spec,
                   pl.BlockSpec((1, DA_HEADS * ln, HEAD_W), lambda b, s: (b, s, 0))]
        a_shapes = [jax.ShapeDtypeStruct(t_shape, BF16), jax.ShapeDtypeStruct((bsz, DA_HEADS, seq, HEAD_W), BF16),
                    jax.ShapeDtypeStruct(t_shape, F32), jax.ShapeDtypeStruct(t_shape, BF16),
                    jax.ShapeDtypeStruct((bsz, DA_HEADS * seq, HEAD_W), F32)]
    else:
        rows = _tok_spec(nb, ln, DA_W)
        a_specs = [rows, rows, rows, pl.BlockSpec((nb, DA_HEADS * ln, HEAD_W), lambda b, s: (b, s, 0)), rows]
        a_shapes = [jax.ShapeDtypeStruct((bsz, DA_HEADS * seq, HEAD_W) if i == 3 else (bsz, seq, DA_W), dt)
                    for i, dt in enumerate((BF16, F32, BF16, F32, BF16))]
    widths = (RET_V_W, D_MODEL, D_MODEL)
    in_specs = [_tok_spec(nb, ln, D_MODEL), _seq_spec(nb), _seq_spec(nb),
                _const_spec((1, D_MODEL)), _const_spec(w_in.shape), tab_spec, tab_spec]
    in_specs += [_const_spec(tab.shape) for tab in tables] + [_const_spec((1, RET_V_W))]
    args = [x, shift, scale, norm_w, w_in, cos, sin, *tables, ret_norm]
    if has_state:
        in_specs.append(state_spec)
        args.append(state)
    return pl.pallas_call(
        functools.partial(_proj_kernel, transposed=transposed, has_state=has_state),
        grid=grid,
        in_specs=in_specs,
        out_specs=a_specs + [_tok_spec(nb, ln, w) for w in widths] + [state_spec],
        out_shape=a_shapes + [jax.ShapeDtypeStruct((bsz, seq, w), BF16) for w in widths] + [state_shape],
        scratch_shapes=[pltpu.VMEM((nb, RET_QK_W, RET_V_W), F32)],
        compiler_params=_params("parallel", "arbitrary"),
        name="proj_t" if transposed else "proj",
    )(*args)


def _lambda(lq1_ref, lk1_ref, lq2_ref, lk2_ref):
    s1 = jnp.sum(lq1_ref[...] * lk1_ref[...], axis=-1, keepdims=True)
    s2 = jnp.sum(lq2_ref[...] * lk2_ref[...], axis=-1, keepdims=True)
    return jnp.exp(s1) - jnp.exp(s2) + LAM_INIT


def _stack_maps(q):
    lane = lax.broadcasted_iota(jnp.int32, q.shape, 1)
    zero = jnp.zeros_like(q)
    return jnp.concatenate([jnp.where(lane < DA_HD, q, zero), jnp.where(lane >= DA_HD, q, zero)], axis=0)


def _prompt_attention(lam, nw_ref, qt_ref, k_ref, vt_ref, o_ref):
    heads, seq = k_ref.shape[1], k_ref.shape[2]
    t = ATTN_T
    feat = lax.broadcasted_iota(jnp.int32, (HEAD_W, t), 0)
    key_chunk = lax.broadcasted_iota(jnp.int32, (t, t), 0) // CHUNK
    query_chunk = lax.broadcasted_iota(jnp.int32, (t, t), 1) // CHUNK
    visible = key_chunk <= query_chunk
    visible2 = jnp.concatenate([visible, visible], axis=1)

    def head_cols(h):
        return slice(h * HEAD_W, (h + 1) * HEAD_W)

    def query_maps(i, h):
        qt = qt_ref[0, head_cols(h), i * t:(i + 1) * t]
        zero = jnp.zeros_like(qt)
        return jnp.concatenate([jnp.where(feat < DA_HD, qt, zero), jnp.where(feat >= DA_HD, qt, zero)], axis=1)

    def scores(i, h):
        lo, hi = i * t, (i + 1) * t
        qm = query_maps(i, h)
        s_diag = jnp.where(visible2, _dot(k_ref[0, h, lo:hi, :], qm), NEG_INF)
        m = jnp.max(s_diag, axis=0, keepdims=True)
        s_full = None
        if i > 0:
            s_full = _dot(k_ref[0, h, :lo, :], qm)
            m = jnp.maximum(m, jnp.max(s_full, axis=0, keepdims=True))
        return s_diag, s_full, m

    ones = jnp.ones((ONES_ROWS, seq), BF16)
    vt1 = [jnp.concatenate([vt_ref[0, head_cols(h), :], ones], axis=0) for h in range(heads)]

    def finish(i, h, s_diag, s_full, m):
        lo, hi = i * t, (i + 1) * t
        acc = _dot(vt1[h][:, lo:hi], jnp.exp2(s_diag - m).astype(BF16))
        if i > 0:
            acc = acc + _dot(vt1[h][:, :lo], jnp.exp2(s_full - m).astype(BF16))
        o2 = acc[:HEAD_W] / acc[HEAD_W:HEAD_W + 1]
        ot = o2[:, :t] - lam * o2[:, t:]
        o_ref[0, h, lo:hi, :] = (_rms(ot.T, nw_ref[h]) * (1.0 - LAM_INIT)).astype(o_ref.dtype)

    n_blocks = seq // t
    pending = [scores(0, h) for h in range(heads)]
    yield
    for i in range(1, n_blocks):
        upcoming = [scores(i, h) for h in range(heads)]
        for h in range(heads):
            finish(i - 1, h, *pending[h])
        pending = upcoming
        yield
    for h in range(heads):
        finish(n_blocks - 1, h, *pending[h])


def _sample_attention(lam, nw_ref, heads, q_ref, kn_ref, vn_ref, kc_ref, vc_ref, o_ref):
    ln = q_ref.shape[1]
    past = kc_ref.shape[-1]
    ones = jnp.ones((past, HEAD_W), BF16)

    def head_cols(j):
        return slice(j * HEAD_W, (j + 1) * HEAD_W)

    def scores(j):
        qs = _stack_maps(q_ref[0, :, head_cols(j)])
        s_cache = _dot(qs, kc_ref[0, j].astype(BF16))
        s_new = _dot_nt(qs, kn_ref[0, :, head_cols(j)])
        m = jnp.maximum(jnp.max(s_cache, axis=-1, keepdims=True), jnp.max(s_new, axis=-1, keepdims=True))
        return s_cache, s_new, m

    def finish(j, s_cache, s_new, m):
        v_cache = vc_ref[0, pl.ds(heads[j], past, stride=DA_HEADS), :].astype(BF16)
        acc = _dot(jnp.exp2(s_cache - m).astype(BF16), jnp.concatenate([v_cache, ones], axis=1))
        v_new = jnp.concatenate([vn_ref[0, :, head_cols(j)], ones[:ln]], axis=1)
        acc = acc + _dot(jnp.exp2(s_new - m).astype(BF16), v_new)
        o2 = acc[:, :HEAD_W] / acc[:, HEAD_W:HEAD_W + 1]
        o = o2[:ln] - lam * o2[ln:]
        o_ref[0, j] = (_rms(o, nw_ref[heads[j]]) * (1.0 - LAM_INIT)).astype(o_ref.dtype)

    pending = scores(0)
    yield
    for j in range(1, len(heads)):
        upcoming = scores(j)
        yield
        finish(j - 1, *pending)
        yield
        pending = upcoming
    finish(len(heads) - 1, *pending)


def _weave(main, side, main_per_side=2):
    done = object()
    main_live = side_live = True
    while main_live or side_live:
        if side_live:
            side_live = next(side, done) is not done
        for _ in range(main_per_side):
            if main_live:
                main_live = next(main, done) is not done


def _attn_kernel(lq1_ref, lk1_ref, lq2_ref, lk2_ref, nwp_ref, nws_ref, qt_ref, k_ref, vt_ref,
                 q_ref, kn_ref, vn_ref, kc_ref, vc_ref, op_ref, os_ref):
    lam = _lambda(lq1_ref, lk1_ref, lq2_ref, lk2_ref)
    g = SAMPLE_HEADS_PER_STEP
    group = pl.program_id(0) % (DA_HEADS // g)
    for grp in range(DA_HEADS // g):
        @pl.when(group == grp)
        def _(grp=grp):
            heads = tuple(range(grp * g, (grp + 1) * g))
            _weave(_prompt_attention(lam, nwp_ref, qt_ref, k_ref, vt_ref, op_ref),
                   _sample_attention(lam, nws_ref, heads, q_ref, kn_ref, vn_ref, kc_ref, vc_ref, os_ref))


def _attention(qt, k, vt, q, k_new, v_new, cache_kt, cache_v4, lambdas, da_norm):
    bp, _, seq, _ = k.shape
    bs, ln, _ = q.shape
    past = cache_kt.shape[-1]
    g = SAMPLE_HEADS_PER_STEP
    groups = DA_HEADS // g
    steps = bp * DA_HEADS
    assert steps == bs * groups, "one prompt (sequence, head) per sample (sequence, head group)"
    assert seq % ATTN_T == 0 and ATTN_T % CHUNK == 0 and past % CHUNK == 0 and ln <= CHUNK
    lam_spec = _const_spec((1, DA_HD))
    t_spec = pl.BlockSpec((1, HEAD_W, seq), lambda i: (i // DA_HEADS, i % DA_HEADS, 0))
    head_spec = pl.BlockSpec((1, 1, seq, HEAD_W), lambda i: (i // DA_HEADS, i % DA_HEADS, 0, 0))
    new_spec = pl.BlockSpec((1, ln, g * HEAD_W), lambda i: (i // groups, 0, i % groups))
    out_new_spec = pl.BlockSpec((1, g, ln, HEAD_W), lambda i: (i // groups, i % groups, 0, 0))
    return pl.pallas_call(
        _attn_kernel,
        grid=(steps,),
        in_specs=[lam_spec] * 4 + [pl.BlockSpec((1, 1, HEAD_W), lambda i: (i % DA_HEADS, 0, 0)),
                                   _const_spec((DA_HEADS, 1, HEAD_W)), t_spec, head_spec, t_spec,
                                   new_spec, new_spec, new_spec,
                                   pl.BlockSpec((1, g, HEAD_W, past), lambda i: (i // groups, i % groups, 0, 0)),
                                   pl.BlockSpec((1, DA_HEADS * past, HEAD_W), lambda i: (i // groups, 0, 0))],
        out_specs=[head_spec, out_new_spec],
        out_shape=[jax.ShapeDtypeStruct((bp, DA_HEADS, seq, HEAD_W), BF16),
                   jax.ShapeDtypeStruct((bs, DA_HEADS, ln, HEAD_W), BF16)],
        compiler_params=_params("parallel"),
        name="attn",
    )(*lambdas, da_norm, da_norm, qt, k, vt, q, k_new, v_new, cache_kt, cache_v4)


def _retention_block(q, k_f32, v, state, dec_ref, qd_ref, kd_ref, sd_ref):
    k = k_f32.astype(BF16)
    cross = _dot(q, state.astype(BF16)) * qd_ref[...]
    lane = lax.broadcasted_iota(jnp.int32, q.shape, 1)
    outs = []
    for h in range(RET_HEADS):
        qh = jnp.where((lane >= h * RET_QK) & (lane < (h + 1) * RET_QK), q, jnp.zeros_like(q))
        scores = _dot_nt(qh, k) * dec_ref[h]
        cols = slice(h * RET_VD, (h + 1) * RET_VD)
        oh = _dot(scores.astype(BF16), v[:, cols]) + cross[:, cols]
        xc = oh - jnp.mean(oh, axis=-1, keepdims=True)
        outs.append(xc * lax.rsqrt(jnp.mean(xc * xc, axis=-1, keepdims=True) + EPS))
    kd = (k_f32 * kd_ref[...]).astype(BF16)
    upd = _dot_tn(kd, v)
    r = lax.broadcasted_iota(jnp.int32, upd.shape, 0) // RET_QK
    cl = lax.broadcasted_iota(jnp.int32, upd.shape, 1) // RET_VD
    new_state = state * sd_ref[...] + jnp.where(r == cl, upd, 0.0)
    return jnp.concatenate(outs, axis=-1), new_state


def _retention_tables(ln):
    log_g = np.log(1.0 - 2.0 ** (-5.0 - np.arange(RET_HEADS, dtype=np.float64)))
    idx = np.arange(ln, dtype=np.float64)
    dist = idx[:, None] - idx[None, :]
    decay = np.where(dist >= 0, np.exp(np.maximum(dist, 0.0)[None] * log_g[:, None, None]), 0.0)
    q_decay = np.exp((idx + 1.0)[:, None] * log_g[None, :])
    k_decay = np.exp((ln - 1.0 - idx)[:, None] * log_g[None, :])
    s_decay = np.exp(ln * log_g)[None, :]
    tables = (decay, np.repeat(q_decay, RET_VD, axis=1), np.repeat(k_decay, RET_QK, axis=1),
              np.repeat(s_decay, RET_VD, axis=1))
    return tuple(jnp.asarray(tab, F32) for tab in tables)


def _merge_ffn_kernel(x_ref, gm_ref, oa_ref, or_ref, ga_ref, gr_ref, wa_ref, wr_ref, wo_ref,
                      sh_ref, sc_ref, g_ref, nw_ref, wup_ref, wdn_ref, nf_ref, o_ref):
    x = x_ref[...]
    nb, ln, _ = x.shape

    def flat(ref):
        return ref[...].reshape(nb * ln, ref.shape[-1])

    o_a = jnp.concatenate([oa_ref[:, hd] for hd in range(DA_HEADS)], axis=-1)
    a = _dot(o_a.reshape(nb * ln, DA_W), wa_ref[...])
    r = _dot(flat(or_ref), wr_ref[...])
    merged = jax.nn.sigmoid(flat(ga_ref).astype(F32)) * a + jax.nn.sigmoid(flat(gr_ref).astype(F32)) * r
    y = _dot(merged.astype(BF16), wo_ref[...]).reshape(nb, ln, D_MODEL)
    x = x + gm_ref[...] * y
    x = _half_step(x, sh_ref, sc_ref, g_ref, nw_ref, wup_ref, wdn_ref)
    o_ref[...] = _rms(x, nf_ref[...])


def _merge_ffn(x, gate_m, o_a, o_r, gate_a, gate_r, w_a, w_r, w_o, shift, scale, gate, norm_w, w_up, w_down,
               norm_final):
    nb, ln, grid = _token_grid(x)
    return pl.pallas_call(
        _merge_ffn_kernel,
        grid=grid,
        in_specs=[_tok_spec(nb, ln, D_MODEL), _seq_spec(nb),
                  pl.BlockSpec((nb, DA_HEADS, ln, HEAD_W), lambda b, s: (b, 0, s, 0)), _tok_spec(nb, ln, RET_V_W),
                  _tok_spec(nb, ln, D_MODEL), _tok_spec(nb, ln, D_MODEL),
                  _const_spec(w_a.shape), _const_spec(w_r.shape), _const_spec(w_o.shape),
                  _seq_spec(nb), _seq_spec(nb), _seq_spec(nb),
                  _const_spec((1, D_MODEL)), _const_spec(w_up.shape), _const_spec(w_down.shape),
                  _const_spec((1, D_MODEL))],
        out_specs=_tok_spec(nb, ln, D_MODEL),
        out_shape=jax.ShapeDtypeStruct(x.shape, F32),
        compiler_params=_params("parallel", "parallel"),
        name="merge_ffn",
    )(x, gate_m, o_a, o_r, gate_a, gate_r, w_a, w_r, w_o, shift, scale, gate, norm_w, w_up, w_down, norm_final)


def _before_attention(x, mod, w, state, offset):
    sh1, sc1, g1, shm, scm = [mod[:, i:i + 1] for i in range(5)]
    x = _ffn(x, sh1, sc1, g1, w["norm_f1"], w["w_up1"], w["w_down1"])
    pos = offset + np.arange(x.shape[1])
    return x, _proj(x, shm, scm, w["norm_mix"], w["w_in"], pos, w["ret_norm"], state, transposed=state is None)


def _after_attention(x, mod, w, norm_final, o_a, o_r, gate_a, gate_r):
    gm, sh2, sc2, g2 = [mod[:, i:i + 1] for i in range(5, N_MOD)]
    return _merge_ffn(x, gm, o_a, o_r, gate_a, gate_r, w["w_a_proj"], w["w_r_proj"], w["w_o"],
                      sh2, sc2, g2, w["norm_f2"], w["w_up2"], w["w_down2"], norm_final)


def kernel(x_prompt, x_sample, c_prompt, c_sample, cache_k, cache_v, state_ret, norm_f1, w_up1, w_down1, norm_mix, w_in, lambda_q1, lambda_k1, lambda_q2, lambda_k2, da_norm, ret_norm, w_a_proj, w_r_proj, w_o, norm_f2, w_up2, w_down2, w_ada, b_ada, norm_final):
    assert cache_k.shape[0] == 1, "single layer"
    n_prompt = c_prompt.shape[0]
    mod = _modulation(jnp.concatenate([c_prompt, c_sample], axis=0), w_ada[0], b_ada[0])
    mod = mod.reshape(-1, N_MOD, D_MODEL)
    w = {
        "norm_f1": norm_f1, "norm_mix": norm_mix, "norm_f2": norm_f2,
        "w_up1": w_up1[0].astype(BF16), "w_down1": w_down1[0].astype(BF16),
        "w_up2": w_up2[0].astype(BF16), "w_down2": w_down2[0].astype(BF16),
        "w_in": w_in[0].astype(BF16), "w_a_proj": w_a_proj[0].astype(BF16),
        "w_r_proj": w_r_proj[0].astype(BF16), "w_o": w_o[0].astype(BF16),
        "lambdas": (lambda_q1, lambda_k1, lambda_q2, lambda_k2),
        "ret_norm": ret_norm[0].reshape(1, RET_V_W),
    }
    nf = norm_final.reshape(1, D_MODEL)
    dec_b, past_len = cache_k.shape[1], cache_k.shape[2]
    cache_kt = jnp.transpose(cache_k[0], (0, 2, 3, 4, 1)).reshape(dec_b, DA_HEADS, HEAD_W, past_len)
    cache_v4 = cache_v[0].reshape(dec_b, past_len * DA_HEADS, HEAD_W)
    mod_p, mod_s = mod[:n_prompt], mod[n_prompt:]
    bp, seq = x_prompt.shape[:2]
    ln = x_sample.shape[1]

    x_s, (q_s, k_s, kb_s, v_s, vb_s, or_s, ga_s, gr_s, st_s) = _before_attention(
        x_sample, mod_s, w, state_ret[0], past_len)
    x_p, (qt_p, kb_p, kt_p, vt_p, v4_p, or_p, ga_p, gr_p, st_p) = _before_attention(x_prompt, mod_p, w, None, 0)
    oa_p, oa_s = _attention(qt_p, kb_p, vt_p, q_s, kb_s, vb_s, cache_kt, cache_v4, w["lambdas"],
                            da_norm[0].reshape(DA_HEADS, 1, HEAD_W))
    y_p = _after_attention(x_p, mod_p, w, nf, oa_p, or_p, ga_p, gr_p)
    y_s = _after_attention(x_s, mod_s, w, nf, oa_s, or_s, ga_s, gr_s)

    k_new_p = jnp.transpose(kt_p.reshape(bp, DA_HEADS, 2, DA_HD, seq), (0, 4, 1, 2, 3))[None]
    v_new_p = v4_p.reshape(1, bp, seq, DA_HEADS, HEAD_W)
    k_new_s = k_s.reshape(1, dec_b, ln, DA_HEADS, 2, DA_HD)
    v_new_s = v_s.reshape(1, dec_b, ln, DA_HEADS, HEAD_W)
    return y_p, y_s, k_new_p, v_new_p, st_p[None], k_new_s, v_new_s, st_s[None]
```

```python
import functools
import math

import jax
import jax.numpy as jnp
import numpy as np
from jax import lax
from jax.experimental import pallas as pl
from jax.experimental.pallas import tpu as pltpu

F32 = jnp.float32
BF16 = jnp.bfloat16

D_MODEL = 1024
D_FF = 2816
N_MOD = 9
CHUNK = 64
DA_HEADS = 4
DA_HD = 64
DA_W = 512
HEAD_W = 128
RET_HEADS = 4
RET_QK = 64
RET_QK_W = 256
RET_VD = 128
RET_V_W = 512
ROPE_BASE = 10000.0
EPS = 1e-6
NEG_INF = -1e30
LAM_INIT = 0.8 - 0.6 * math.exp(-0.3 * 0)
LOG2_E = math.log2(math.e)
ONES_ROWS = 16

_IN_SIZES = (DA_W, DA_W, DA_W, RET_QK_W, RET_QK_W, RET_V_W, RET_V_W, D_MODEL, D_MODEL)
_IN_OFF = [0]
for _s in _IN_SIZES:
    _IN_OFF.append(_IN_OFF[-1] + _s)
IN_WIDTH = _IN_OFF[-1]

TOKEN_TILE = 512
FFN_TOKEN_TILE = 1024
BF16_ROW_TILE = 16
ATTN_T = 256
SAMPLE_HEADS_PER_STEP = 2
RET_BLOCK = 256
V7X_VMEM_BYTES = 64 * 1024 * 1024
VMEM_LIMIT = V7X_VMEM_BYTES * 7 // 8


def _const_spec(shape):
    nd = len(shape)
    return pl.BlockSpec(shape, lambda *_: (0,) * nd, pipeline_mode=pl.Buffered(1))


def _params(*sem):
    return pltpu.CompilerParams(dimension_semantics=sem, vmem_limit_bytes=VMEM_LIMIT)


def _rms(x, w):
    ms = jnp.mean(x * x, axis=-1, keepdims=True)
    return x * lax.rsqrt(ms + EPS) * w


def _silu(x):
    return x * jax.nn.sigmoid(x)


def _dot(a, b):
    return jnp.dot(a, b, preferred_element_type=F32)


def _dot_nt(a, b):
    return lax.dot_general(a, b, (((1,), (1,)), ((), ())), preferred_element_type=F32)


def _dot_tn(a, b):
    return lax.dot_general(a, b, (((0,), (0,)), ((), ())), preferred_element_type=F32)


def _mod_kernel(c_ref, w_ref, b_ref, o_ref):
    s = _silu(c_ref[...]).astype(BF16)
    o_ref[...] = _dot(s, w_ref[...].astype(BF16)) + b_ref[...]


def _modulation(c, w_ada, b_ada):
    n, width = c.shape[0], w_ada.shape[1]
    tn = width // 8
    return pl.pallas_call(
        _mod_kernel,
        grid=(width // tn,),
        in_specs=[_const_spec((n, D_MODEL)),
                  pl.BlockSpec((D_MODEL, tn), lambda j: (0, j)),
                  pl.BlockSpec((1, tn), lambda j: (0, j))],
        out_specs=pl.BlockSpec((n, tn), lambda j: (0, j)),
        out_shape=jax.ShapeDtypeStruct((n, width), F32),
        compiler_params=_params("parallel"),
        name="mod",
    )(c, w_ada, b_ada.reshape(1, width))


def _token_grid(x, tile=TOKEN_TILE):
    bsz, seq, _ = x.shape
    ln = min(seq, tile)
    nb = tile // ln
    assert seq % ln == 0 and bsz % nb == 0
    return nb, ln, (bsz // nb, seq // ln)


def _tok_spec(nb, ln, width):
    return pl.BlockSpec((nb, ln, width), lambda b, s: (b, s, 0))


def _seq_spec(nb):
    return pl.BlockSpec((nb, 1, D_MODEL), lambda b, s: (b, 0, 0))


def _half_step(x, sh_ref, sc_ref, g_ref, nw_ref, wup_ref, wdn_ref):
    nb, ln, _ = x.shape
    h = _rms(x, nw_ref[...]) * (1.0 + sc_ref[...]) + sh_ref[...]
    hb = h.reshape(nb * ln, D_MODEL).astype(BF16)
    a = _dot(hb, wup_ref[:, :D_FF])
    b = _dot(hb, wup_ref[:, D_FF:])
    act = (_silu(a) * b).astype(BF16)
    y = _dot(act, wdn_ref[...]).reshape(nb, ln, D_MODEL)
    return x + 0.5 * g_ref[...] * y


def _ffn_kernel(*refs, cast_steps, grid_steps):
    n = len(cast_steps)
    x_ref, sh_ref, sc_ref, g_ref, nw_ref, wup_ref, wdn_ref = refs[:7]
    src_refs, o_ref, dst_refs = refs[7:7 + n], refs[7 + n], refs[8 + n:]
    o_ref[...] = _half_step(x_ref[...], sh_ref, sc_ref, g_ref, nw_ref, wup_ref, wdn_ref)
    step = pl.program_id(0) * pl.num_programs(1) + pl.program_id(1)
    for src_ref, dst_ref, steps in zip(src_refs, dst_refs, cast_steps):
        def cast_chunk(src_ref=src_ref, dst_ref=dst_ref):
            dst_ref[...] = src_ref[...].astype(BF16)
        if steps == grid_steps:
            cast_chunk()
        else:
            pl.when(step < steps)(cast_chunk)


def _cast_chunks(rows, n_steps):
    return max(d for d in range(1, n_steps + 1) if rows % d == 0 and (rows // d) % BF16_ROW_TILE == 0)


def _ffn(x, shift, scale, gate, norm_w, w_up, w_down, cast=(), tile=TOKEN_TILE):
    nb, ln, grid = _token_grid(x, tile)
    gs = grid[1]
    cast_steps = tuple(_cast_chunks(w.shape[0], grid[0] * gs) for w in cast)

    def chunk_spec(w, steps):
        return pl.BlockSpec((w.shape[0] // steps, w.shape[1]), lambda b, s: (jnp.minimum(b * gs + s, steps - 1), 0))

    chunk_specs = [chunk_spec(w, steps) for w, steps in zip(cast, cast_steps)]
    outs = pl.pallas_call(
        functools.partial(_ffn_kernel, cast_steps=cast_steps, grid_steps=grid[0] * gs),
        grid=grid,
        in_specs=[_tok_spec(nb, ln, D_MODEL), _seq_spec(nb), _seq_spec(nb), _seq_spec(nb),
                  _const_spec((1, D_MODEL)), _const_spec(w_up.shape), _const_spec(w_down.shape)] + chunk_specs,
        out_specs=[_tok_spec(nb, ln, D_MODEL)] + chunk_specs,
        out_shape=[jax.ShapeDtypeStruct(x.shape, F32)] + [jax.ShapeDtypeStruct(w.shape, BF16) for w in cast],
        compiler_params=_params("arbitrary", "arbitrary"),
        name="ffn",
    )(x, shift, scale, gate, norm_w, w_up, w_down, *cast)
    return outs[0], outs[1:]


def _rotary(x, cos, sin_signed):
    lane = lax.broadcasted_iota(jnp.int32, x.shape, 1)
    first_half = (lane % RET_QK) < (RET_QK // 2)
    width = x.shape[1]
    partner = jnp.where(first_half, pltpu.roll(x, width - RET_QK // 2, 1), pltpu.roll(x, RET_QK // 2, 1))
    return x * cos + partner * sin_signed


def _proj_kernel(*refs, transposed, has_state):
    (x_ref, sh_ref, sc_ref, nw_ref, win_ref, cos_ref, sin_ref,
     dec_ref, qd_ref, kd_ref, sd_ref, rn_ref), refs = refs[:12], refs[12:]
    if has_state:
        s0_ref, refs = refs[0], refs[1:]
    a_refs, (or_ref, ga_ref, gr_ref, sout_ref, st_ref) = refs[:5], refs[5:]
    x = x_ref[...]
    nb, ln, _ = x.shape
    h = _rms(x, nw_ref[...]) * (1.0 + sc_ref[...]) + sh_ref[...]
    hb = h.reshape(nb * ln, D_MODEL).astype(BF16)

    def proj(i):
        return _dot(hb, win_ref[:, _IN_OFF[i]:_IN_OFF[i + 1]])

    def put(ref, val):
        ref[...] = val.reshape(nb, ln, val.shape[-1]).astype(ref.dtype)

    q = proj(0) * (DA_HD ** -0.5 * LOG2_E)
    k = proj(1)
    v = proj(2)
    if transposed:
        qt_ref, kb_ref, kt_ref, vt_ref, v4_ref = a_refs
        qt_ref[0] = q.T.astype(BF16)
        kt_ref[0] = k.T
        vt_ref[0] = v.T.astype(BF16)
        for hd in range(DA_HEADS):
            kb_ref[0, hd] = k[:, hd * HEAD_W:(hd + 1) * HEAD_W].astype(BF16)
            v4_ref[0, pl.ds(hd, ln, stride=DA_HEADS), :] = v[:, hd * HEAD_W:(hd + 1) * HEAD_W]
    else:
        q_ref, k_ref, kb_ref, v_ref, vb_ref = a_refs
        put(q_ref, q)
        for n in range(nb):
            for hd in range(DA_HEADS):
                for c in range(2):
                    lo = hd * HEAD_W + c * DA_HD
                    k_ref[n, :, hd, c, :] = k[n * ln:(n + 1) * ln, lo:lo + DA_HD]
        put(kb_ref, k)
        for n in range(nb):
            for hd in range(DA_HEADS):
                v_ref[n, pl.ds(hd, ln, stride=DA_HEADS), :] = v[n * ln:(n + 1) * ln, hd * HEAD_W:(hd + 1) * HEAD_W]
        put(vb_ref, v)
    put(ga_ref, proj(7))
    put(gr_ref, proj(8))

    seq_step = pl.program_id(1)

    @pl.when(seq_step == 0)
    def _():
        st_ref[...] = jnp.zeros(st_ref.shape, F32)
        if has_state:
            for n in range(nb):
                for hd in range(RET_HEADS):
                    st_ref[n, hd * RET_QK:(hd + 1) * RET_QK, hd * RET_VD:(hd + 1) * RET_VD] = s0_ref[n, hd]

    cos, sin_signed = cos_ref[...], sin_ref[...]
    q_r = _rotary(proj(3), cos, sin_signed).astype(BF16)
    k_r = _rotary(proj(4), cos, sin_signed) * (RET_QK ** -0.5)
    v_r = proj(5).astype(BF16)
    z_r = proj(6)
    rl = dec_ref.shape[-1]
    for n in range(nb):
        for blk in range(ln // rl):
            rows = slice(n * ln + blk * rl, n * ln + (blk + 1) * rl)
            o, st_ref[n] = _retention_block(q_r[rows], k_r[rows], v_r[rows], st_ref[n],
                                            dec_ref, qd_ref, kd_ref, sd_ref)
            o = o * rn_ref[...] * _silu(z_r[rows])
            or_ref[n, blk * rl:(blk + 1) * rl, :] = o.astype(or_ref.dtype)

    @pl.when(seq_step == pl.num_programs(1) - 1)
    def _():
        for n in range(nb):
            for hd in range(RET_HEADS):
                sout_ref[n, hd] = st_ref[n, hd * RET_QK:(hd + 1) * RET_QK, hd * RET_VD:(hd + 1) * RET_VD]


def _rope_tables(pos, nb):
    half = RET_QK // 2
    inv = ROPE_BASE ** (-np.arange(half, dtype=np.float64) / half)
    ang = np.asarray(pos, np.float64)[:, None] * inv[None, :]
    cos, sin = np.cos(ang), np.sin(ang)
    cos = np.tile(np.concatenate([cos, cos], axis=-1), (nb, RET_HEADS))
    sin = np.tile(np.concatenate([-sin, sin], axis=-1), (nb, RET_HEADS))
    return jnp.asarray(cos, F32), jnp.asarray(sin, F32)


def _proj(x, shift, scale, norm_w, w_in, pos, ret_norm, state, transposed):
    bsz, seq, _ = x.shape
    nb, ln, grid = _token_grid(x)
    cos, sin = _rope_tables(pos, nb)
    tab_spec = pl.BlockSpec((nb * ln, RET_QK_W), lambda b, s: (s, 0))
    rl = min(ln, RET_BLOCK)
    assert ln % rl == 0
    tables = _retention_tables(rl)
    has_state = state is not None
    state_spec = pl.BlockSpec((nb, RET_HEADS, RET_QK, RET_VD), lambda b, s: (b, 0, 0, 0))
    state_shape = jax.ShapeDtypeStruct((bsz, RET_HEADS, RET_QK, RET_VD), F32)
    if transposed:
        assert nb == 1
        t_spec = pl.BlockSpec((1, DA_W, ln), lambda b, s: (b, 0, s))
        t_shape = (bsz, DA_W, seq)
        a_specs = [t_spec, pl.BlockSpec((1, DA_HEADS, ln, HEAD_W), lambda b, s: (b, 0, s, 0)), t_spec, t_spec,
                   pl.BlockSpec((1, DA_HEADS * ln, HEAD_W), lambda b, s: (b, s, 0))]
        a_shapes = [jax.ShapeDtypeStruct(t_shape, BF16), jax.ShapeDtypeStruct((bsz, DA_HEADS, seq, HEAD_W), BF16),
                    jax.ShapeDtypeStruct(t_shape, F32), jax.ShapeDtypeStruct(t_shape, BF16),
                    jax.ShapeDtypeStruct((bsz, DA_HEADS * seq, HEAD_W), F32)]
    else:
        rows = _tok_spec(nb, ln, DA_W)
        a_specs = [rows, pl.BlockSpec((nb, ln, DA_HEADS, 2, DA_HD), lambda b, s: (b, s, 0, 0, 0)), rows,
                   pl.BlockSpec((nb, DA_HEADS * ln, HEAD_W), lambda b, s: (b, s, 0)), rows]
        a_shapes = [jax.ShapeDtypeStruct((bsz, seq, DA_W), BF16),
                    jax.ShapeDtypeStruct((bsz, seq, DA_HEADS, 2, DA_HD), F32),
                    jax.ShapeDtypeStruct((bsz, seq, DA_W), BF16),
                    jax.ShapeDtypeStruct((bsz, DA_HEADS * seq, HEAD_W), F32),
                    jax.ShapeDtypeStruct((bsz, seq, DA_W), BF16)]
    widths = (RET_V_W, D_MODEL, D_MODEL)
    in_specs = [_tok_spec(nb, ln, D_MODEL), _seq_spec(nb), _seq_spec(nb),
                _const_spec((1, D_MODEL)), _const_spec(w_in.shape), tab_spec, tab_spec]
    in_specs += [_const_spec(tab.shape) for tab in tables] + [_const_spec((1, RET_V_W))]
    args = [x, shift, scale, norm_w, w_in, cos, sin, *tables, ret_norm]
    if has_state:
        in_specs.append(state_spec)
        args.append(state)
    return pl.pallas_call(
        functools.partial(_proj_kernel, transposed=transposed, has_state=has_state),
        grid=grid,
        in_specs=in_specs,
        out_specs=a_specs + [_tok_spec(nb, ln, w) for w in widths] + [state_spec],
        out_shape=a_shapes + [jax.ShapeDtypeStruct((bsz, seq, w), BF16) for w in widths] + [state_shape],
        scratch_shapes=[pltpu.VMEM((nb, RET_QK_W, RET_V_W), F32)],
        compiler_params=_params("parallel", "arbitrary"),
        name="proj_t" if transposed else "proj",
    )(*args)


def _lambda(lq1_ref, lk1_ref, lq2_ref, lk2_ref):
    s1 = jnp.sum(lq1_ref[...] * lk1_ref[...], axis=-1, keepdims=True)
    s2 = jnp.sum(lq2_ref[...] * lk2_ref[...], axis=-1, keepdims=True)
    return jnp.exp(s1) - jnp.exp(s2) + LAM_INIT


def _stack_maps(q):
    lane = lax.broadcasted_iota(jnp.int32, q.shape, 1)
    zero = jnp.zeros_like(q)
    return jnp.concatenate([jnp.where(lane < DA_HD, q, zero), jnp.where(lane >= DA_HD, q, zero)], axis=0)


def _prompt_attention(lam, nw_ref, qt_ref, k_ref, vt_ref, o_ref):
    heads, seq = k_ref.shape[1], k_ref.shape[2]
    t = ATTN_T
    feat = lax.broadcasted_iota(jnp.int32, (HEAD_W, t), 0)
    key_chunk = lax.broadcasted_iota(jnp.int32, (t, t), 0) // CHUNK
    query_chunk = lax.broadcasted_iota(jnp.int32, (t, t), 1) // CHUNK
    visible = key_chunk <= query_chunk
    visible2 = jnp.concatenate([visible, visible], axis=1)

    def head_cols(h):
        return slice(h * HEAD_W, (h + 1) * HEAD_W)

    def query_maps(i, h):
        qt = qt_ref[0, head_cols(h), i * t:(i + 1) * t]
        zero = jnp.zeros_like(qt)
        return jnp.concatenate([jnp.where(feat < DA_HD, qt, zero), jnp.where(feat >= DA_HD, qt, zero)], axis=1)

    def scores(i, h):
        lo, hi = i * t, (i + 1) * t
        qm = query_maps(i, h)
        s_diag = jnp.where(visible2, _dot(k_ref[0, h, lo:hi, :], qm), NEG_INF)
        m = jnp.max(s_diag, axis=0, keepdims=True)
        s_full = None
        if i > 0:
            s_full = _dot(k_ref[0, h, :lo, :], qm)
            m = jnp.maximum(m, jnp.max(s_full, axis=0, keepdims=True))
        return s_diag, s_full, m

    ones = jnp.ones((ONES_ROWS, seq), BF16)
    vt1 = [jnp.concatenate([vt_ref[0, head_cols(h), :], ones], axis=0) for h in range(heads)]

    def finish(i, h, s_diag, s_full, m):
        lo, hi = i * t, (i + 1) * t
        acc = _dot(vt1[h][:, lo:hi], jnp.exp2(s_diag - m).astype(BF16))
        if i > 0:
            acc = acc + _dot(vt1[h][:, :lo], jnp.exp2(s_full - m).astype(BF16))
        o2 = acc[:HEAD_W] / acc[HEAD_W:HEAD_W + 1]
        ot = o2[:, :t] - lam * o2[:, t:]
        o_ref[0, h, lo:hi, :] = (_rms(ot.T, nw_ref[h]) * (1.0 - LAM_INIT)).astype(o_ref.dtype)

    n_blocks = seq // t
    pending = [scores(0, h) for h in range(heads)]
    yield
    for i in range(1, n_blocks):
        upcoming = [scores(i, h) for h in range(heads)]
        for h in range(heads):
            finish(i - 1, h, *pending[h])
        pending = upcoming
        yield
    for h in range(heads):
        finish(n_blocks - 1, h, *pending[h])


def _sample_attention(lam, nw_ref, heads, q_ref, kn_ref, vn_ref, kc_ref, vc_ref, o_ref):
    ln = q_ref.shape[1]
    past = kc_ref.shape[-1]
    ones = jnp.ones((past, HEAD_W), BF16)

    def head_cols(j):
        return slice(j * HEAD_W, (j + 1) * HEAD_W)

    def scores(j):
        qs = _stack_maps(q_ref[0, :, head_cols(j)])
        s_cache = _dot(qs, kc_ref[0, j].astype(BF16))
        s_new = _dot_nt(qs, kn_ref[0, :, head_cols(j)])
        m = jnp.maximum(jnp.max(s_cache, axis=-1, keepdims=True), jnp.max(s_new, axis=-1, keepdims=True))
        return s_cache, s_new, m

    def finish(j, s_cache, s_new, m):
        v_cache = vc_ref[0, pl.ds(heads[j], past, stride=DA_HEADS), :].astype(BF16)
        acc = _dot(jnp.exp2(s_cache - m).astype(BF16), jnp.concatenate([v_cache, ones], axis=1))
        v_new = jnp.concatenate([vn_ref[0, :, head_cols(j)], ones[:ln]], axis=1)
        acc = acc + _dot(jnp.exp2(s_new - m).astype(BF16), v_new)
        o2 = acc[:, :HEAD_W] / acc[:, HEAD_W:HEAD_W + 1]
        o = o2[:ln] - lam * o2[ln:]
        o_ref[0, j] = (_rms(o, nw_ref[heads[j]]) * (1.0 - LAM_INIT)).astype(o_ref.dtype)

    pending = scores(0)
    yield
    for j in range(1, len(heads)):
        upcoming = scores(j)
        yield
        finish(j - 1, *pending)
        yield
        pending = upcoming
    finish(len(heads) - 1, *pending)


def _weave(main, side, main_per_side=2):
    done = object()
    main_live = side_live = True
    while main_live or side_live:
        if side_live:
            side_live = next(side, done) is not done
        for _ in range(main_per_side):
            if main_live:
                main_live = next(main, done) is not done


def _attn_kernel(lq1_ref, lk1_ref, lq2_ref, lk2_ref, nwp_ref, nws_ref, qt_ref, k_ref, vt_ref,
                 q_ref, kn_ref, vn_ref, kc_ref, vc_ref, op_ref, os_ref):
    lam = _lambda(lq1_ref, lk1_ref, lq2_ref, lk2_ref)
    g = SAMPLE_HEADS_PER_STEP
    group = pl.program_id(0) % (DA_HEADS // g)
    for grp in range(DA_HEADS // g):
        @pl.when(group == grp)
        def _(grp=grp):
            heads = tuple(range(grp * g, (grp + 1) * g))
            _weave(_prompt_attention(lam, nwp_ref, qt_ref, k_ref, vt_ref, op_ref),
                   _sample_attention(lam, nws_ref, heads, q_ref, kn_ref, vn_ref, kc_ref, vc_ref, os_ref))


def _attention(qt, k, vt, q, k_new, v_new, cache_kt, cache_v4, lambdas, da_norm):
    bp, _, seq, _ = k.shape
    bs, ln, _ = q.shape
    past = cache_kt.shape[-1]
    g = SAMPLE_HEADS_PER_STEP
    groups = DA_HEADS // g
    steps = bp * DA_HEADS
    assert steps == bs * groups, "one prompt (sequence, head) per sample (sequence, head group)"
    assert seq % ATTN_T == 0 and ATTN_T % CHUNK == 0 and past % CHUNK == 0 and ln <= CHUNK
    lam_spec = _const_spec((1, DA_HD))
    t_spec = pl.BlockSpec((1, HEAD_W, seq), lambda i: (i // DA_HEADS, i % DA_HEADS, 0))
    head_spec = pl.BlockSpec((1, 1, seq, HEAD_W), lambda i: (i // DA_HEADS, i % DA_HEADS, 0, 0))
    new_spec = pl.BlockSpec((1, ln, g * HEAD_W), lambda i: (i // groups, 0, i % groups))
    out_new_spec = pl.BlockSpec((1, g, ln, HEAD_W), lambda i: (i // groups, i % groups, 0, 0))
    return pl.pallas_call(
        _attn_kernel,
        grid=(steps,),
        in_specs=[lam_spec] * 4 + [pl.BlockSpec((1, 1, HEAD_W), lambda i: (i % DA_HEADS, 0, 0)),
                                   _const_spec((DA_HEADS, 1, HEAD_W)), t_spec, head_spec, t_spec,
                                   new_spec, new_spec, new_spec,
                                   pl.BlockSpec((1, g, HEAD_W, past), lambda i: (i // groups, i % groups, 0, 0)),
                                   pl.BlockSpec((1, DA_HEADS * past, HEAD_W), lambda i: (i // groups, 0, 0))],
        out_specs=[head_spec, out_new_spec],
        out_shape=[jax.ShapeDtypeStruct((bp, DA_HEADS, seq, HEAD_W), BF16),
                   jax.ShapeDtypeStruct((bs, DA_HEADS, ln, HEAD_W), BF16)],
        compiler_params=_params("parallel"),
        name="attn",
    )(*lambdas, da_norm, da_norm, qt, k, vt, q, k_new, v_new, cache_kt, cache_v4)


def _retention_block(q, k_f32, v, state, dec_ref, qd_ref, kd_ref, sd_ref):
    k = k_f32.astype(BF16)
    cross = _dot(q, state.astype(BF16)) * qd_ref[...]
    lane = lax.broadcasted_iota(jnp.int32, q.shape, 1)
    outs = []
    for h in range(RET_HEADS):
        qh = jnp.where((lane >= h * RET_QK) & (lane < (h + 1) * RET_QK), q, jnp.zeros_like(q))
        scores = _dot_nt(qh, k) * dec_ref[h]
        cols = slice(h * RET_VD, (h + 1) * RET_VD)
        oh = _dot(scores.astype(BF16), v[:, cols]) + cross[:, cols]
        xc = oh - jnp.mean(oh, axis=-1, keepdims=True)
        outs.append(xc * lax.rsqrt(jnp.mean(xc * xc, axis=-1, keepdims=True) + EPS))
    kd = (k_f32 * kd_ref[...]).astype(BF16)
    upd = _dot_tn(kd, v)
    r = lax.broadcasted_iota(jnp.int32, upd.shape, 0) // RET_QK
    cl = lax.broadcasted_iota(jnp.int32, upd.shape, 1) // RET_VD
    new_state = state * sd_ref[...] + jnp.where(r == cl, upd, 0.0)
    return jnp.concatenate(outs, axis=-1), new_state


def _retention_tables(ln):
    log_g = np.log(1.0 - 2.0 ** (-5.0 - np.arange(RET_HEADS, dtype=np.float64)))
    idx = np.arange(ln, dtype=np.float64)
    dist = idx[:, None] - idx[None, :]
    decay = np.where(dist >= 0, np.exp(np.maximum(dist, 0.0)[None] * log_g[:, None, None]), 0.0)
    q_decay = np.exp((idx + 1.0)[:, None] * log_g[None, :])
    k_decay = np.exp((ln - 1.0 - idx)[:, None] * log_g[None, :])
    s_decay = np.exp(ln * log_g)[None, :]
    tables = (decay, np.repeat(q_decay, RET_VD, axis=1), np.repeat(k_decay, RET_QK, axis=1),
              np.repeat(s_decay, RET_VD, axis=1))
    return tuple(jnp.asarray(tab, F32) for tab in tables)


def _merge_ffn_kernel(x_ref, gm_ref, oa_ref, or_ref, ga_ref, gr_ref, wa_ref, wr_ref, wo_ref,
                      sh_ref, sc_ref, g_ref, nw_ref, wup_ref, wdn_ref, nf_ref, o_ref):
    x = x_ref[...]
    nb, ln, _ = x.shape

    def flat(ref):
        return ref[...].reshape(nb * ln, ref.shape[-1])

    o_a = jnp.concatenate([oa_ref[:, hd] for hd in range(DA_HEADS)], axis=-1)
    a = _dot(o_a.reshape(nb * ln, DA_W), wa_ref[...])
    r = _dot(flat(or_ref), wr_ref[...])
    merged = jax.nn.sigmoid(flat(ga_ref).astype(F32)) * a + jax.nn.sigmoid(flat(gr_ref).astype(F32)) * r
    y = _dot(merged.astype(BF16), wo_ref[...]).reshape(nb, ln, D_MODEL)
    x = x + gm_ref[...] * y
    x = _half_step(x, sh_ref, sc_ref, g_ref, nw_ref, wup_ref, wdn_ref)
    o_ref[...] = _rms(x, nf_ref[...])


def _merge_ffn(x, gate_m, o_a, o_r, gate_a, gate_r, w_a, w_r, w_o, shift, scale, gate, norm_w, w_up, w_down,
               norm_final):
    nb, ln, grid = _token_grid(x)
    return pl.pallas_call(
        _merge_ffn_kernel,
        grid=grid,
        in_specs=[_tok_spec(nb, ln, D_MODEL), _seq_spec(nb),
                  pl.BlockSpec((nb, DA_HEADS, ln, HEAD_W), lambda b, s: (b, 0, s, 0)), _tok_spec(nb, ln, RET_V_W),
                  _tok_spec(nb, ln, D_MODEL), _tok_spec(nb, ln, D_MODEL),
                  _const_spec(w_a.shape), _const_spec(w_r.shape), _const_spec(w_o.shape),
                  _seq_spec(nb), _seq_spec(nb), _seq_spec(nb),
                  _const_spec((1, D_MODEL)), _const_spec(w_up.shape), _const_spec(w_down.shape),
                  _const_spec((1, D_MODEL))],
        out_specs=_tok_spec(nb, ln, D_MODEL),
        out_shape=jax.ShapeDtypeStruct(x.shape, F32),
        compiler_params=_params("parallel", "parallel"),
        name="merge_ffn",
    )(x, gate_m, o_a, o_r, gate_a, gate_r, w_a, w_r, w_o, shift, scale, gate, norm_w, w_up, w_down, norm_final)


def _first_half_step(x, mod, w, cast, tile):
    sh1, sc1, g1 = [mod[:, i:i + 1] for i in range(3)]
    return _ffn(x, sh1, sc1, g1, w["norm_f1"], w["w_up1"], w["w_down1"], cast, tile)


def _mixer_inputs(x, mod, w, state, offset):
    shm, scm = [mod[:, i:i + 1] for i in range(3, 5)]
    pos = offset + np.arange(x.shape[1])
    return _proj(x, shm, scm, w["norm_mix"], w["w_in"], pos, w["ret_norm"], state, transposed=state is None)


def _after_attention(x, mod, w, norm_final, o_a, o_r, gate_a, gate_r):
    gm, sh2, sc2, g2 = [mod[:, i:i + 1] for i in range(5, N_MOD)]
    return _merge_ffn(x, gm, o_a, o_r, gate_a, gate_r, w["w_a_proj"], w["w_r_proj"], w["w_o"],
                      sh2, sc2, g2, w["norm_f2"], w["w_up2"], w["w_down2"], norm_final)


def kernel(x_prompt, x_sample, c_prompt, c_sample, cache_k, cache_v, state_ret, norm_f1, w_up1, w_down1, norm_mix, w_in, lambda_q1, lambda_k1, lambda_q2, lambda_k2, da_norm, ret_norm, w_a_proj, w_r_proj, w_o, norm_f2, w_up2, w_down2, w_ada, b_ada, norm_final):
    assert cache_k.shape[0] == 1, "single layer"
    n_prompt = c_prompt.shape[0]
    mod = _modulation(jnp.concatenate([c_prompt, c_sample], axis=0), w_ada[0], b_ada[0])
    mod = mod.reshape(-1, N_MOD, D_MODEL)
    w = {
        "norm_f1": norm_f1, "norm_mix": norm_mix, "norm_f2": norm_f2,
        "w_up1": w_up1[0].astype(BF16), "w_down1": w_down1[0].astype(BF16),
        "lambdas": (lambda_q1, lambda_k1, lambda_q2, lambda_k2),
        "ret_norm": ret_norm[0].reshape(1, RET_V_W),
    }
    nf = norm_final.reshape(1, D_MODEL)
    dec_b, past_len = cache_k.shape[1], cache_k.shape[2]
    cache_kt = jnp.transpose(cache_k[0], (0, 2, 3, 4, 1)).reshape(dec_b, DA_HEADS, HEAD_W, past_len)
    cache_v4 = cache_v[0].reshape(dec_b, past_len * DA_HEADS, HEAD_W)
    mod_p, mod_s = mod[:n_prompt], mod[n_prompt:]
    bp, seq = x_prompt.shape[:2]
    ln = x_sample.shape[1]

    x_s, (w["w_in"],) = _first_half_step(x_sample, mod_s, w, (w_in[0],), TOKEN_TILE)
    q_s, k_s, kb_s, v_s, vb_s, or_s, ga_s, gr_s, st_s = _mixer_inputs(x_s, mod_s, w, state_ret[0], past_len)
    later = (w_a_proj[0], w_r_proj[0], w_o[0], w_up2[0], w_down2[0])
    x_p, (w["w_a_proj"], w["w_r_proj"], w["w_o"], w["w_up2"], w["w_down2"]) = _first_half_step(
        x_prompt, mod_p, w, later, FFN_TOKEN_TILE)
    qt_p, kb_p, kt_p, vt_p, v4_p, or_p, ga_p, gr_p, st_p = _mixer_inputs(x_p, mod_p, w, None, 0)
    oa_p, oa_s = _attention(qt_p, kb_p, vt_p, q_s, kb_s, vb_s, cache_kt, cache_v4, w["lambdas"],
                            da_norm[0].reshape(DA_HEADS, 1, HEAD_W))
    y_p = _after_attention(x_p, mod_p, w, nf, oa_p, or_p, ga_p, gr_p)
    y_s = _after_attention(x_s, mod_s, w, nf, oa_s, or_s, ga_s, gr_s)

    k_new_p = jnp.transpose(kt_p.reshape(bp, DA_HEADS, 2, DA_HD, seq), (0, 4, 1, 2, 3))[None]
    v_new_p = v4_p.reshape(1, bp, seq, DA_HEADS, HEAD_W)
    k_new_s = k_s[None]
    v_new_s = v_s.reshape(1, dec_b, ln, DA_HEADS, HEAD_W)
    return y_p, y_s, k_new_p, v_new_p, st_p[None], k_new_s, v_new_s, st_s[None]
```

```python
import functools
import math

import jax
import jax.numpy as jnp
import numpy as np
from jax import lax
from jax.experimental import pallas as pl
from jax.experimental.pallas import tpu as pltpu

F32 = jnp.float32
BF16 = jnp.bfloat16

D_MODEL = 1024
D_FF = 2816
N_MOD = 9
CHUNK = 64
DA_HEADS = 4
DA_HD = 64
DA_W = 512
HEAD_W = 128
RET_HEADS = 4
RET_QK = 64
RET_QK_W = 256
RET_VD = 128
RET_V_W = 512
ROPE_BASE = 10000.0
EPS = 1e-6
NEG_INF = -1e30
LAM_INIT = 0.8 - 0.6 * math.exp(-0.3 * 0)
LOG2_E = math.log2(math.e)
ONES_ROWS = 16

_IN_SIZES = (DA_W, DA_W, DA_W, RET_QK_W, RET_QK_W, RET_V_W, RET_V_W, D_MODEL, D_MODEL)
_IN_OFF = [0]
for _s in _IN_SIZES:
    _IN_OFF.append(_IN_OFF[-1] + _s)

TOKEN_TILE = 512
FFN_TOKEN_TILE = 1024
BF16_ROW_TILE = 16
ATTN_T = 256
SAMPLE_HEADS_PER_STEP = 2
RET_BLOCK = 256
V7X_VMEM_BYTES = 64 * 1024 * 1024
VMEM_LIMIT = V7X_VMEM_BYTES * 7 // 8


def _const_spec(shape):
    nd = len(shape)
    return pl.BlockSpec(shape, lambda *_: (0,) * nd, pipeline_mode=pl.Buffered(1))


def _params(*sem):
    return pltpu.CompilerParams(dimension_semantics=sem, vmem_limit_bytes=VMEM_LIMIT)


def _rms(x, w):
    ms = jnp.mean(x * x, axis=-1, keepdims=True)
    return x * lax.rsqrt(ms + EPS) * w


def _sigmoid(x):
    return 0.5 * (1.0 + jnp.tanh(0.5 * x))


def _silu(x):
    return x * _sigmoid(x)


def _dot(a, b):
    return jnp.dot(a, b, preferred_element_type=F32)


def _dot_nt(a, b):
    return lax.dot_general(a, b, (((1,), (1,)), ((), ())), preferred_element_type=F32)


def _dot_tn(a, b):
    return lax.dot_general(a, b, (((0,), (0,)), ((), ())), preferred_element_type=F32)


def _mod_kernel(c_ref, w_ref, b_ref, o_ref):
    s = _silu(c_ref[...]).astype(BF16)
    o_ref[...] = _dot(s, w_ref[...].astype(BF16)) + b_ref[...]


def _modulation(c, w_ada, b_ada):
    n, width = c.shape[0], w_ada.shape[1]
    tn = width // 8
    return pl.pallas_call(
        _mod_kernel,
        grid=(width // tn,),
        in_specs=[_const_spec((n, D_MODEL)),
                  pl.BlockSpec((D_MODEL, tn), lambda j: (0, j)),
                  pl.BlockSpec((1, tn), lambda j: (0, j))],
        out_specs=pl.BlockSpec((n, tn), lambda j: (0, j)),
        out_shape=jax.ShapeDtypeStruct((n, width), F32),
        compiler_params=_params("parallel"),
        name="mod",
    )(c, w_ada, b_ada.reshape(1, width))


def _token_grid(x, tile=TOKEN_TILE):
    bsz, seq, _ = x.shape
    ln = min(seq, tile)
    nb = tile // ln
    assert seq % ln == 0 and bsz % nb == 0
    return nb, ln, (bsz // nb, seq // ln)


def _tok_spec(nb, ln, width):
    return pl.BlockSpec((nb, ln, width), lambda b, s: (b, s, 0))


def _seq_spec(nb):
    return pl.BlockSpec((nb, 1, D_MODEL), lambda b, s: (b, 0, 0))


def _half_step(x, sh_ref, sc_ref, g_ref, nw_ref, wup_ref, wdn_ref):
    nb, ln, _ = x.shape
    h = _rms(x, nw_ref[...]) * (1.0 + sc_ref[...]) + sh_ref[...]
    hb = h.reshape(nb * ln, D_MODEL).astype(BF16)
    a = _dot(hb, wup_ref[:, :D_FF])
    b = _dot(hb, wup_ref[:, D_FF:])
    act = (_silu(a) * b).astype(BF16)
    y = _dot(act, wdn_ref[...]).reshape(nb, ln, D_MODEL)
    return x + 0.5 * g_ref[...] * y


def _ffn_kernel(*refs, cast_steps, grid_steps):
    n = len(cast_steps)
    x_ref, sh_ref, sc_ref, g_ref, nw_ref, wup_ref, wdn_ref = refs[:7]
    src_refs, o_ref, dst_refs = refs[7:7 + n], refs[7 + n], refs[8 + n:]
    o_ref[...] = _half_step(x_ref[...], sh_ref, sc_ref, g_ref, nw_ref, wup_ref, wdn_ref)
    step = pl.program_id(0) * pl.num_programs(1) + pl.program_id(1)
    for src_ref, dst_ref, steps in zip(src_refs, dst_refs, cast_steps):
        def cast_chunk(src_ref=src_ref, dst_ref=dst_ref):
            dst_ref[...] = src_ref[...].astype(BF16)
        if steps == grid_steps:
            cast_chunk()
        else:
            pl.when(step < steps)(cast_chunk)


def _cast_chunks(rows, n_steps):
    return max(d for d in range(1, n_steps + 1) if rows % d == 0 and (rows // d) % BF16_ROW_TILE == 0)


def _ffn(x, shift, scale, gate, norm_w, w_up, w_down, cast=(), tile=TOKEN_TILE):
    nb, ln, grid = _token_grid(x, tile)
    gs = grid[1]
    cast_steps = tuple(_cast_chunks(w.shape[0], grid[0] * gs) for w in cast)

    def chunk_spec(w, steps):
        return pl.BlockSpec((w.shape[0] // steps, w.shape[1]), lambda b, s: (jnp.minimum(b * gs + s, steps - 1), 0))

    chunk_specs = [chunk_spec(w, steps) for w, steps in zip(cast, cast_steps)]
    outs = pl.pallas_call(
        functools.partial(_ffn_kernel, cast_steps=cast_steps, grid_steps=grid[0] * gs),
        grid=grid,
        in_specs=[_tok_spec(nb, ln, D_MODEL), _seq_spec(nb), _seq_spec(nb), _seq_spec(nb),
                  _const_spec((1, D_MODEL)), _const_spec(w_up.shape), _const_spec(w_down.shape)] + chunk_specs,
        out_specs=[_tok_spec(nb, ln, D_MODEL)] + chunk_specs,
        out_shape=[jax.ShapeDtypeStruct(x.shape, F32)] + [jax.ShapeDtypeStruct(w.shape, BF16) for w in cast],
        compiler_params=_params("arbitrary", "arbitrary"),
        name="ffn",
    )(x, shift, scale, gate, norm_w, w_up, w_down, *cast)
    return outs[0], outs[1:]


def _rotary(x, cos, sin_signed):
    lane = lax.broadcasted_iota(jnp.int32, x.shape, 1)
    first_half = (lane % RET_QK) < (RET_QK // 2)
    width = x.shape[1]
    partner = jnp.where(first_half, pltpu.roll(x, width - RET_QK // 2, 1), pltpu.roll(x, RET_QK // 2, 1))
    return x * cos + partner * sin_signed


def _proj_kernel(*refs, transposed, has_state):
    (x_ref, sh_ref, sc_ref, nw_ref, win_ref, cos_ref, sin_ref,
     dec_ref, qd_ref, kd_ref, sd_ref, rn_ref), refs = refs[:12], refs[12:]
    if has_state:
        s0_ref, refs = refs[0], refs[1:]
    a_refs, (or_ref, ga_ref, gr_ref, sout_ref, st_ref) = refs[:5], refs[5:]
    x = x_ref[...]
    nb, ln, _ = x.shape
    h = _rms(x, nw_ref[...]) * (1.0 + sc_ref[...]) + sh_ref[...]
    hb = h.reshape(nb * ln, D_MODEL).astype(BF16)

    def proj(i):
        return _dot(hb, win_ref[:, _IN_OFF[i]:_IN_OFF[i + 1]])

    def put(ref, val):
        ref[...] = val.reshape(nb, ln, val.shape[-1]).astype(ref.dtype)

    q = proj(0) * (DA_HD ** -0.5 * LOG2_E)
    k = proj(1)
    v = proj(2)
    if transposed:
        qt_ref, kb_ref, kt_ref, vt_ref, v4_ref = a_refs
        qt_ref[0] = q.T.astype(BF16)
        kt_ref[0] = k.T
        vt_ref[0] = v.T.astype(BF16)
        for hd in range(DA_HEADS):
            kb_ref[0, hd] = k[:, hd * HEAD_W:(hd + 1) * HEAD_W].astype(BF16)
            v4_ref[0, pl.ds(hd, ln, stride=DA_HEADS), :] = v[:, hd * HEAD_W:(hd + 1) * HEAD_W]
    else:
        q_ref, k_ref, kb_ref, v_ref, vb_ref = a_refs
        put(q_ref, q)
        for n in range(nb):
            for hd in range(DA_HEADS):
                for c in range(2):
                    lo = hd * HEAD_W + c * DA_HD
                    k_ref[n, :, hd, c, :] = k[n * ln:(n + 1) * ln, lo:lo + DA_HD]
        put(kb_ref, k)
        for n in range(nb):
            for hd in range(DA_HEADS):
                v_ref[n, pl.ds(hd, ln, stride=DA_HEADS), :] = v[n * ln:(n + 1) * ln, hd * HEAD_W:(hd + 1) * HEAD_W]
        put(vb_ref, v)
    put(ga_ref, proj(7))
    put(gr_ref, proj(8))

    seq_step = pl.program_id(1)

    @pl.when(seq_step == 0)
    def _():
        st_ref[...] = jnp.zeros(st_ref.shape, F32)
        if has_state:
            for n in range(nb):
                for hd in range(RET_HEADS):
                    st_ref[n, hd * RET_QK:(hd + 1) * RET_QK, hd * RET_VD:(hd + 1) * RET_VD] = s0_ref[n, hd]

    cos, sin_signed = cos_ref[...], sin_ref[...]
    q_r = _rotary(proj(3), cos, sin_signed).astype(BF16)
    k_r = _rotary(proj(4), cos, sin_signed) * (RET_QK ** -0.5)
    v_r = proj(5).astype(BF16)
    z_r = proj(6)
    rl = dec_ref.shape[-1]
    for n in range(nb):
        for blk in range(ln // rl):
            rows = slice(n * ln + blk * rl, n * ln + (blk + 1) * rl)
            o, st_ref[n] = _retention_block(q_r[rows], k_r[rows], v_r[rows], st_ref[n],
                                            dec_ref, qd_ref, kd_ref, sd_ref)
            o = o * rn_ref[...] * _silu(z_r[rows])
            or_ref[n, blk * rl:(blk + 1) * rl, :] = o.astype(or_ref.dtype)

    @pl.when(seq_step == pl.num_programs(1) - 1)
    def _():
        for n in range(nb):
            for hd in range(RET_HEADS):
                sout_ref[n, hd] = st_ref[n, hd * RET_QK:(hd + 1) * RET_QK, hd * RET_VD:(hd + 1) * RET_VD]


def _rope_tables(pos, nb):
    half = RET_QK // 2
    inv = ROPE_BASE ** (-np.arange(half, dtype=np.float64) / half)
    ang = np.asarray(pos, np.float64)[:, None] * inv[None, :]
    cos, sin = np.cos(ang), np.sin(ang)
    cos = np.tile(np.concatenate([cos, cos], axis=-1), (nb, RET_HEADS))
    sin = np.tile(np.concatenate([-sin, sin], axis=-1), (nb, RET_HEADS))
    return jnp.asarray(cos, F32), jnp.asarray(sin, F32)


def _proj(x, shift, scale, norm_w, w_in, pos, ret_norm, state, transposed):
    bsz, seq, _ = x.shape
    nb, ln, grid = _token_grid(x)
    cos, sin = _rope_tables(pos, nb)
    tab_spec = pl.BlockSpec((nb * ln, RET_QK_W), lambda b, s: (s, 0))
    rl = min(ln, RET_BLOCK)
    assert ln % rl == 0
    tables = _retention_tables(rl)
    has_state = state is not None
    state_spec = pl.BlockSpec((nb, RET_HEADS, RET_QK, RET_VD), lambda b, s: (b, 0, 0, 0))
    state_shape = jax.ShapeDtypeStruct((bsz, RET_HEADS, RET_QK, RET_VD), F32)
    if transposed:
        assert nb == 1
        t_spec = pl.BlockSpec((1, DA_W, ln), lambda b, s: (b, 0, s))
        t_shape = (bsz, DA_W, seq)
        a_specs = [t_spec, pl.BlockSpec((1, DA_HEADS, ln, HEAD_W), lambda b, s: (b, 0, s, 0)), t_spec, t_spec,
                   pl.BlockSpec((1, DA_HEADS * ln, HEAD_W), lambda b, s: (b, s, 0))]
        a_shapes = [jax.ShapeDtypeStruct(t_shape, BF16), jax.ShapeDtypeStruct((bsz, DA_HEADS, seq, HEAD_W), BF16),
                    jax.ShapeDtypeStruct(t_shape, F32), jax.ShapeDtypeStruct(t_shape, BF16),
                    jax.ShapeDtypeStruct((bsz, DA_HEADS * seq, HEAD_W), F32)]
    else:
        rows = _tok_spec(nb, ln, DA_W)
        a_specs = [rows, pl.BlockSpec((nb, ln, DA_HEADS, 2, DA_HD), lambda b, s: (b, s, 0, 0, 0)), rows,
                   pl.BlockSpec((nb, DA_HEADS * ln, HEAD_W), lambda b, s: (b, s, 0)), rows]
        a_shapes = [jax.ShapeDtypeStruct((bsz, seq, DA_W), BF16),
                    jax.ShapeDtypeStruct((bsz, seq, DA_HEADS, 2, DA_HD), F32),
                    jax.ShapeDtypeStruct((bsz, seq, DA_W), BF16),
                    jax.ShapeDtypeStruct((bsz, DA_HEADS * seq, HEAD_W), F32),
                    jax.ShapeDtypeStruct((bsz, seq, DA_W), BF16)]
    widths = (RET_V_W, D_MODEL, D_MODEL)
    in_specs = [_tok_spec(nb, ln, D_MODEL), _seq_spec(nb), _seq_spec(nb),
                _const_spec((1, D_MODEL)), _const_spec(w_in.shape), tab_spec, tab_spec]
    in_specs += [_const_spec(tab.shape) for tab in tables] + [_const_spec((1, RET_V_W))]
    args = [x, shift, scale, norm_w, w_in, cos, sin, *tables, ret_norm]
    if has_state:
        in_specs.append(state_spec)
        args.append(state)
    return pl.pallas_call(
        functools.partial(_proj_kernel, transposed=transposed, has_state=has_state),
        grid=grid,
        in_specs=in_specs,
        out_specs=a_specs + [_tok_spec(nb, ln, w) for w in widths] + [state_spec],
        out_shape=a_shapes + [jax.ShapeDtypeStruct((bsz, seq, w), BF16) for w in widths] + [state_shape],
        scratch_shapes=[pltpu.VMEM((nb, RET_QK_W, RET_V_W), F32)],
        compiler_params=_params("parallel", "arbitrary"),
        name="proj_t" if transposed else "proj",
    )(*args)


def _lambda(lq1_ref, lk1_ref, lq2_ref, lk2_ref):
    s1 = jnp.sum(lq1_ref[...] * lk1_ref[...], axis=-1, keepdims=True)
    s2 = jnp.sum(lq2_ref[...] * lk2_ref[...], axis=-1, keepdims=True)
    return jnp.exp(s1) - jnp.exp(s2) + LAM_INIT


def _stack_maps(q):
    lane = lax.broadcasted_iota(jnp.int32, q.shape, 1)
    zero = jnp.zeros_like(q)
    return jnp.concatenate([jnp.where(lane < DA_HD, q, zero), jnp.where(lane >= DA_HD, q, zero)], axis=0)


def _prompt_attention(lam, nw_ref, qt_ref, k_ref, vt_ref, o_ref):
    heads, seq = k_ref.shape[1], k_ref.shape[2]
    t = ATTN_T
    feat = lax.broadcasted_iota(jnp.int32, (HEAD_W, t), 0)
    key_chunk = lax.broadcasted_iota(jnp.int32, (t, t), 0) // CHUNK
    query_chunk = lax.broadcasted_iota(jnp.int32, (t, t), 1) // CHUNK
    visible = key_chunk <= query_chunk
    visible2 = jnp.concatenate([visible, visible], axis=1)

    def head_cols(h):
        return slice(h * HEAD_W, (h + 1) * HEAD_W)

    def query_maps(i, h):
        qt = qt_ref[0, head_cols(h), i * t:(i + 1) * t]
        zero = jnp.zeros_like(qt)
        return jnp.concatenate([jnp.where(feat < DA_HD, qt, zero), jnp.where(feat >= DA_HD, qt, zero)], axis=1)

    def scores(i, h):
        lo, hi = i * t, (i + 1) * t
        qm = query_maps(i, h)
        s_diag = jnp.where(visible2, _dot(k_ref[0, h, lo:hi, :], qm), NEG_INF)
        m = jnp.max(s_diag, axis=0, keepdims=True)
        s_full = None
        if i > 0:
            s_full = _dot(k_ref[0, h, :lo, :], qm)
            m = jnp.maximum(m, jnp.max(s_full, axis=0, keepdims=True))
        return s_diag, s_full, m

    ones = jnp.ones((ONES_ROWS, seq), BF16)
    vt1 = [jnp.concatenate([vt_ref[0, head_cols(h), :], ones], axis=0) for h in range(heads)]

    def finish(i, h, s_diag, s_full, m):
        lo, hi = i * t, (i + 1) * t
        acc = _dot(vt1[h][:, lo:hi], jnp.exp2(s_diag - m).astype(BF16))
        if i > 0:
            acc = acc + _dot(vt1[h][:, :lo], jnp.exp2(s_full - m).astype(BF16))
        o2 = acc[:HEAD_W] / acc[HEAD_W:HEAD_W + 1]
        ot = o2[:, :t] - lam * o2[:, t:]
        o_ref[0, h, lo:hi, :] = (_rms(ot.T, nw_ref[h]) * (1.0 - LAM_INIT)).astype(o_ref.dtype)

    n_blocks = seq // t
    pending = [scores(0, h) for h in range(heads)]
    yield
    for i in range(1, n_blocks):
        upcoming = [scores(i, h) for h in range(heads)]
        for h in range(heads):
            finish(i - 1, h, *pending[h])
        pending = upcoming
        yield
    for h in range(heads):
        finish(n_blocks - 1, h, *pending[h])


def _sample_attention(lam, nw_ref, heads, q_ref, kn_ref, vn_ref, kc_ref, vc_ref, o_ref):
    ln = q_ref.shape[1]
    past = kc_ref.shape[-1]
    ones = jnp.ones((past, HEAD_W), BF16)

    def head_cols(j):
        return slice(j * HEAD_W, (j + 1) * HEAD_W)

    def scores(j):
        qs = _stack_maps(q_ref[0, :, head_cols(j)])
        s_cache = _dot(qs, kc_ref[0, j].astype(BF16))
        s_new = _dot_nt(qs, kn_ref[0, :, head_cols(j)])
        m = jnp.maximum(jnp.max(s_cache, axis=-1, keepdims=True), jnp.max(s_new, axis=-1, keepdims=True))
        return s_cache, s_new, m

    def finish(j, s_cache, s_new, m):
        v_cache = vc_ref[0, pl.ds(heads[j], past, stride=DA_HEADS), :].astype(BF16)
        acc = _dot(jnp.exp2(s_cache - m).astype(BF16), jnp.concatenate([v_cache, ones], axis=1))
        v_new = jnp.concatenate([vn_ref[0, :, head_cols(j)], ones[:ln]], axis=1)
        acc = acc + _dot(jnp.exp2(s_new - m).astype(BF16), v_new)
        o2 = acc[:, :HEAD_W] / acc[:, HEAD_W:HEAD_W + 1]
        o = o2[:ln] - lam * o2[ln:]
        o_ref[0, j] = (_rms(o, nw_ref[j]) * (1.0 - LAM_INIT)).astype(o_ref.dtype)

    pending = scores(0)
    yield
    for j in range(1, len(heads)):
        upcoming = scores(j)
        yield
        finish(j - 1, *pending)
        yield
        pending = upcoming
    finish(len(heads) - 1, *pending)


def _weave(main, side, main_per_side=2):
    done = object()
    main_live = side_live = True
    while main_live or side_live:
        if side_live:
            side_live = next(side, done) is not done
        for _ in range(main_per_side):
            if main_live:
                main_live = next(main, done) is not done


def _attn_kernel(lq1_ref, lk1_ref, lq2_ref, lk2_ref, nwp_ref, nws_ref, qt_ref, k_ref, vt_ref,
                 q_ref, kn_ref, vn_ref, kc_ref, vc_ref, op_ref, os_ref):
    lam = _lambda(lq1_ref, lk1_ref, lq2_ref, lk2_ref)
    g = SAMPLE_HEADS_PER_STEP
    first_head = (pl.program_id(0) % (DA_HEADS // g)) * g
    heads = tuple(first_head + j for j in range(g))
    _weave(_prompt_attention(lam, nwp_ref, qt_ref, k_ref, vt_ref, op_ref),
           _sample_attention(lam, nws_ref, heads, q_ref, kn_ref, vn_ref, kc_ref, vc_ref, os_ref))


def _attention(qt, k, vt, q, k_new, v_new, cache_kt, cache_v4, lambdas, da_norm):
    bp, _, seq, _ = k.shape
    bs, ln, _ = q.shape
    past = cache_kt.shape[-1]
    g = SAMPLE_HEADS_PER_STEP
    groups = DA_HEADS // g
    steps = bp * DA_HEADS
    assert steps == bs * groups, "one prompt (sequence, head) per sample (sequence, head group)"
    assert seq % ATTN_T == 0 and ATTN_T % CHUNK == 0 and past % CHUNK == 0 and ln <= CHUNK
    lam_spec = _const_spec((1, DA_HD))
    t_spec = pl.BlockSpec((1, HEAD_W, seq), lambda i: (i // DA_HEADS, i % DA_HEADS, 0))
    head_spec = pl.BlockSpec((1, 1, seq, HEAD_W), lambda i: (i // DA_HEADS, i % DA_HEADS, 0, 0))
    new_spec = pl.BlockSpec((1, ln, g * HEAD_W), lambda i: (i // groups, 0, i % groups))
    out_new_spec = pl.BlockSpec((1, g, ln, HEAD_W), lambda i: (i // groups, i % groups, 0, 0))
    return pl.pallas_call(
        _attn_kernel,
        grid=(steps,),
        in_specs=[lam_spec] * 4 + [pl.BlockSpec((1, 1, HEAD_W), lambda i: (i % DA_HEADS, 0, 0)),
                                   pl.BlockSpec((g, 1, HEAD_W), lambda i: (i % groups, 0, 0)), t_spec, head_spec, t_spec,
                                   new_spec, new_spec, new_spec,
                                   pl.BlockSpec((1, g, HEAD_W, past), lambda i: (i // groups, i % groups, 0, 0)),
                                   pl.BlockSpec((1, DA_HEADS * past, HEAD_W), lambda i: (i // groups, 0, 0))],
        out_specs=[head_spec, out_new_spec],
        out_shape=[jax.ShapeDtypeStruct((bp, DA_HEADS, seq, HEAD_W), BF16),
                   jax.ShapeDtypeStruct((bs, DA_HEADS, ln, HEAD_W), BF16)],
        compiler_params=_params("parallel"),
        name="attn",
    )(*lambdas, da_norm, da_norm, qt, k, vt, q, k_new, v_new, cache_kt, cache_v4)


def _retention_block(q, k_f32, v, state, dec_ref, qd_ref, kd_ref, sd_ref):
    k = k_f32.astype(BF16)
    cross = _dot(q, state.astype(BF16)) * qd_ref[...]
    lane = lax.broadcasted_iota(jnp.int32, q.shape, 1)
    outs = []
    for h in range(RET_HEADS):
        qh = jnp.where((lane >= h * RET_QK) & (lane < (h + 1) * RET_QK), q, jnp.zeros_like(q))
        scores = _dot_nt(qh, k) * dec_ref[h]
        cols = slice(h * RET_VD, (h + 1) * RET_VD)
        oh = _dot(scores.astype(BF16), v[:, cols]) + cross[:, cols]
        xc = oh - jnp.mean(oh, axis=-1, keepdims=True)
        outs.append(xc * lax.rsqrt(jnp.mean(xc * xc, axis=-1, keepdims=True) + EPS))
    kd = (k_f32 * kd_ref[...]).astype(BF16)
    upd = _dot_tn(kd, v)
    r = lax.broadcasted_iota(jnp.int32, upd.shape, 0) // RET_QK
    cl = lax.broadcasted_iota(jnp.int32, upd.shape, 1) // RET_VD
    new_state = state * sd_ref[...] + jnp.where(r == cl, upd, 0.0)
    return jnp.concatenate(outs, axis=-1), new_state


def _retention_tables(ln):
    log_g = np.log(1.0 - 2.0 ** (-5.0 - np.arange(RET_HEADS, dtype=np.float64)))
    idx = np.arange(ln, dtype=np.float64)
    dist = idx[:, None] - idx[None, :]
    decay = np.where(dist >= 0, np.exp(np.maximum(dist, 0.0)[None] * log_g[:, None, None]), 0.0)
    q_decay = np.exp((idx + 1.0)[:, None] * log_g[None, :])
    k_decay = np.exp((ln - 1.0 - idx)[:, None] * log_g[None, :])
    s_decay = np.exp(ln * log_g)[None, :]
    tables = (decay, np.repeat(q_decay, RET_VD, axis=1), np.repeat(k_decay, RET_QK, axis=1),
              np.repeat(s_decay, RET_VD, axis=1))
    return tuple(jnp.asarray(tab, F32) for tab in tables)


def _merge_ffn_kernel(x_ref, gm_ref, oa_ref, or_ref, ga_ref, gr_ref, wa_ref, wr_ref, wo_ref,
                      sh_ref, sc_ref, g_ref, nw_ref, wup_ref, wdn_ref, nf_ref, o_ref):
    x = x_ref[...]
    nb, ln, _ = x.shape

    def flat(ref):
        return ref[...].reshape(nb * ln, ref.shape[-1])

    o_a = jnp.concatenate([oa_ref[:, hd] for hd in range(DA_HEADS)], axis=-1)
    a = _dot(o_a.reshape(nb * ln, DA_W), wa_ref[...])
    r = _dot(flat(or_ref), wr_ref[...])
    merged = _sigmoid(flat(ga_ref).astype(F32)) * a + _sigmoid(flat(gr_ref).astype(F32)) * r
    y = _dot(merged.astype(BF16), wo_ref[...]).reshape(nb, ln, D_MODEL)
    x = x + gm_ref[...] * y
    x = _half_step(x, sh_ref, sc_ref, g_ref, nw_ref, wup_ref, wdn_ref)
    o_ref[...] = _rms(x, nf_ref[...])


def _merge_ffn(x, gate_m, o_a, o_r, gate_a, gate_r, w_a, w_r, w_o, shift, scale, gate, norm_w, w_up, w_down,
               norm_final):
    nb, ln, grid = _token_grid(x)
    return pl.pallas_call(
        _merge_ffn_kernel,
        grid=grid,
        in_specs=[_tok_spec(nb, ln, D_MODEL), _seq_spec(nb),
                  pl.BlockSpec((nb, DA_HEADS, ln, HEAD_W), lambda b, s: (b, 0, s, 0)), _tok_spec(nb, ln, RET_V_W),
                  _tok_spec(nb, ln, D_MODEL), _tok_spec(nb, ln, D_MODEL),
                  _const_spec(w_a.shape), _const_spec(w_r.shape), _const_spec(w_o.shape),
                  _seq_spec(nb), _seq_spec(nb), _seq_spec(nb),
                  _const_spec((1, D_MODEL)), _const_spec(w_up.shape), _const_spec(w_down.shape),
                  _const_spec((1, D_MODEL))],
        out_specs=_tok_spec(nb, ln, D_MODEL),
        out_shape=jax.ShapeDtypeStruct(x.shape, F32),
        compiler_params=_params("parallel", "parallel"),
        name="merge_ffn",
    )(x, gate_m, o_a, o_r, gate_a, gate_r, w_a, w_r, w_o, shift, scale, gate, norm_w, w_up, w_down, norm_final)


def _first_half_step(x, mod, w, cast, tile):
    sh1, sc1, g1 = [mod[:, i:i + 1] for i in range(3)]
    return _ffn(x, sh1, sc1, g1, w["norm_f1"], w["w_up1"], w["w_down1"], cast, tile)


def _mixer_inputs(x, mod, w, state, offset):
    shm, scm = [mod[:, i:i + 1] for i in range(3, 5)]
    pos = offset + np.arange(x.shape[1])
    return _proj(x, shm, scm, w["norm_mix"], w["w_in"], pos, w["ret_norm"], state, transposed=state is None)


def _after_attention(x, mod, w, norm_final, o_a, o_r, gate_a, gate_r):
    gm, sh2, sc2, g2 = [mod[:, i:i + 1] for i in range(5, N_MOD)]
    return _merge_ffn(x, gm, o_a, o_r, gate_a, gate_r, w["w_a_proj"], w["w_r_proj"], w["w_o"],
                      sh2, sc2, g2, w["norm_f2"], w["w_up2"], w["w_down2"], norm_final)


def kernel(x_prompt, x_sample, c_prompt, c_sample, cache_k, cache_v, state_ret, norm_f1, w_up1, w_down1, norm_mix, w_in, lambda_q1, lambda_k1, lambda_q2, lambda_k2, da_norm, ret_norm, w_a_proj, w_r_proj, w_o, norm_f2, w_up2, w_down2, w_ada, b_ada, norm_final):
    assert cache_k.shape[0] == 1, "single layer"
    n_prompt = c_prompt.shape[0]
    mod = _modulation(jnp.concatenate([c_prompt, c_sample], axis=0), w_ada[0], b_ada[0])
    mod = mod.reshape(-1, N_MOD, D_MODEL)
    w = {
        "norm_f1": norm_f1, "norm_mix": norm_mix, "norm_f2": norm_f2,
        "w_up1": w_up1[0].astype(BF16), "w_down1": w_down1[0].astype(BF16),
        "lambdas": (lambda_q1, lambda_k1, lambda_q2, lambda_k2),
        "ret_norm": ret_norm[0].reshape(1, RET_V_W),
    }
    nf = norm_final.reshape(1, D_MODEL)
    dec_b, past_len = cache_k.shape[1], cache_k.shape[2]
    cache_kt = jnp.transpose(cache_k[0], (0, 2, 3, 4, 1)).reshape(dec_b, DA_HEADS, HEAD_W, past_len)
    cache_v4 = cache_v[0].reshape(dec_b, past_len * DA_HEADS, HEAD_W)
    mod_p, mod_s = mod[:n_prompt], mod[n_prompt:]
    bp, seq = x_prompt.shape[:2]
    ln = x_sample.shape[1]

    x_s, (w["w_in"],) = _first_half_step(x_sample, mod_s, w, (w_in[0],), TOKEN_TILE)
    q_s, k_s, kb_s, v_s, vb_s, or_s, ga_s, gr_s, st_s = _mixer_inputs(x_s, mod_s, w, state_ret[0], past_len)
    later = (w_a_proj[0], w_r_proj[0], w_o[0], w_up2[0], w_down2[0])
    x_p, (w["w_a_proj"], w["w_r_proj"], w["w_o"], w["w_up2"], w["w_down2"]) = _first_half_step(
        x_prompt, mod_p, w, later, FFN_TOKEN_TILE)
    qt_p, kb_p, kt_p, vt_p, v4_p, or_p, ga_p, gr_p, st_p = _mixer_inputs(x_p, mod_p, w, None, 0)
    oa_p, oa_s = _attention(qt_p, kb_p, vt_p, q_s, kb_s, vb_s, cache_kt, cache_v4, w["lambdas"],
                            da_norm[0].reshape(DA_HEADS, 1, HEAD_W))
    y_p = _after_attention(x_p, mod_p, w, nf, oa_p, or_p, ga_p, gr_p)
    y_s = _after_attention(x_s, mod_s, w, nf, oa_s, or_s, ga_s, gr_s)

    k_new_p = jnp.transpose(kt_p.reshape(bp, DA_HEADS, 2, DA_HD, seq), (0, 4, 1, 2, 3))[None]
    v_new_p = v4_p.reshape(1, bp, seq, DA_HEADS, HEAD_W)
    k_new_s = k_s[None]
    v_new_s = v_s.reshape(1, dec_b, ln, DA_HEADS, HEAD_W)
    return y_p, y_s, k_new_p, v_new_p, st_p[None], k_new_s, v_new_s, st_s[None]
```

```python
import functools
import math

import jax
import jax.numpy as jnp
import numpy as np
from jax import lax
from jax.experimental import pallas as pl
from jax.experimental.pallas import tpu as pltpu

F32 = jnp.float32
BF16 = jnp.bfloat16

D_MODEL = 1024
D_FF = 2816
N_MOD = 9
CHUNK = 64
DA_HEADS = 4
DA_HD = 64
DA_W = 512
HEAD_W = 128
RET_HEADS = 4
RET_QK = 64
RET_QK_W = 256
RET_VD = 128
RET_V_W = 512
ROPE_BASE = 10000.0
EPS = 1e-6
NEG_INF = -1e30
LAM_INIT = 0.8 - 0.6 * math.exp(-0.3 * 0)
LOG2_E = math.log2(math.e)
ONES_ROWS = 16

_IN_SIZES = (DA_W, DA_W, DA_W, RET_QK_W, RET_QK_W, RET_V_W, RET_V_W, D_MODEL, D_MODEL)
_IN_OFF = [0]
for _s in _IN_SIZES:
    _IN_OFF.append(_IN_OFF[-1] + _s)

TOKEN_TILE = 512
FFN_TOKEN_TILE = 1024
BF16_ROW_TILE = 16
ATTN_T = 256
SAMPLE_HEADS_PER_STEP = 2
RET_BLOCK = 256
V7X_VMEM_BYTES = 64 * 1024 * 1024
VMEM_LIMIT = V7X_VMEM_BYTES * 7 // 8


def _const_spec(shape):
    nd = len(shape)
    return pl.BlockSpec(shape, lambda *_: (0,) * nd, pipeline_mode=pl.Buffered(1))


def _params(*sem):
    return pltpu.CompilerParams(dimension_semantics=sem, vmem_limit_bytes=VMEM_LIMIT)


def _rms(x, w):
    ms = jnp.mean(x * x, axis=-1, keepdims=True)
    return x * lax.rsqrt(ms + EPS) * w


def _sigmoid(x):
    return 0.5 * (1.0 + jnp.tanh(0.5 * x))


def _silu(x):
    return x * _sigmoid(x)


def _dot(a, b):
    return jnp.dot(a, b, preferred_element_type=F32)


def _dot_nt(a, b):
    return lax.dot_general(a, b, (((1,), (1,)), ((), ())), preferred_element_type=F32)


def _dot_tn(a, b):
    return lax.dot_general(a, b, (((0,), (0,)), ((), ())), preferred_element_type=F32)


def _mod_kernel(c_ref, w_ref, b_ref, o_ref):
    s = _silu(c_ref[...]).astype(BF16)
    o_ref[...] = _dot(s, w_ref[...].astype(BF16)) + b_ref[...]


def _modulation(c, w_ada, b_ada):
    n, width = c.shape[0], w_ada.shape[1]
    tn = width // 8
    return pl.pallas_call(
        _mod_kernel,
        grid=(width // tn,),
        in_specs=[_const_spec((n, D_MODEL)),
                  pl.BlockSpec((D_MODEL, tn), lambda j: (0, j)),
                  pl.BlockSpec((1, tn), lambda j: (0, j))],
        out_specs=pl.BlockSpec((n, tn), lambda j: (0, j)),
        out_shape=jax.ShapeDtypeStruct((n, width), F32),
        compiler_params=_params("parallel"),
        name="mod",
    )(c, w_ada, b_ada.reshape(1, width))


def _token_grid(x, tile=TOKEN_TILE):
    bsz, seq, _ = x.shape
    ln = min(seq, tile)
    nb = tile // ln
    assert seq % ln == 0 and bsz % nb == 0
    return nb, ln, (bsz // nb, seq // ln)


def _tok_spec(nb, ln, width):
    return pl.BlockSpec((nb, ln, width), lambda b, s: (b, s, 0))


def _seq_spec(nb):
    return pl.BlockSpec((nb, 1, D_MODEL), lambda b, s: (b, 0, 0))


def _half_step(x, sh_ref, sc_ref, g_ref, nw_ref, wup_ref, wdn_ref):
    nb, ln, _ = x.shape
    h = _rms(x, nw_ref[...] * (1.0 + sc_ref[...])) + sh_ref[...]
    hb = h.reshape(nb * ln, D_MODEL).astype(BF16)
    a = _dot(hb, wup_ref[:, :D_FF])
    b = _dot(hb, wup_ref[:, D_FF:])
    act = (_silu(a) * b).astype(BF16)
    y = _dot(act, wdn_ref[...]).reshape(nb, ln, D_MODEL)
    return x + 0.5 * g_ref[...] * y


def _ffn_kernel(*refs, cast_steps, grid_steps):
    n = len(cast_steps)
    x_ref, sh_ref, sc_ref, g_ref, nw_ref, wup_ref, wdn_ref = refs[:7]
    src_refs, o_ref, dst_refs = refs[7:7 + n], refs[7 + n], refs[8 + n:]
    o_ref[...] = _half_step(x_ref[...], sh_ref, sc_ref, g_ref, nw_ref, wup_ref, wdn_ref)
    step = pl.program_id(0) * pl.num_programs(1) + pl.program_id(1)
    for src_ref, dst_ref, steps in zip(src_refs, dst_refs, cast_steps):
        def cast_chunk(src_ref=src_ref, dst_ref=dst_ref):
            dst_ref[...] = src_ref[...].astype(BF16)
        if steps == grid_steps:
            cast_chunk()
        else:
            pl.when(step < steps)(cast_chunk)


def _cast_chunks(rows, n_steps):
    return max(d for d in range(1, n_steps + 1) if rows % d == 0 and (rows // d) % BF16_ROW_TILE == 0)


def _ffn(x, shift, scale, gate, norm_w, w_up, w_down, cast=(), tile=TOKEN_TILE):
    nb, ln, grid = _token_grid(x, tile)
    gs = grid[1]
    cast_steps = tuple(_cast_chunks(w.shape[0], grid[0] * gs) for w in cast)

    def chunk_spec(w, steps):
        return pl.BlockSpec((w.shape[0] // steps, w.shape[1]), lambda b, s: (jnp.minimum(b * gs + s, steps - 1), 0))

    chunk_specs = [chunk_spec(w, steps) for w, steps in zip(cast, cast_steps)]
    outs = pl.pallas_call(
        functools.partial(_ffn_kernel, cast_steps=cast_steps, grid_steps=grid[0] * gs),
        grid=grid,
        in_specs=[_tok_spec(nb, ln, D_MODEL), _seq_spec(nb), _seq_spec(nb), _seq_spec(nb),
                  _const_spec((1, D_MODEL)), _const_spec(w_up.shape), _const_spec(w_down.shape)] + chunk_specs,
        out_specs=[_tok_spec(nb, ln, D_MODEL)] + chunk_specs,
        out_shape=[jax.ShapeDtypeStruct(x.shape, F32)] + [jax.ShapeDtypeStruct(w.shape, BF16) for w in cast],
        compiler_params=_params("arbitrary", "arbitrary"),
        name="ffn",
    )(x, shift, scale, gate, norm_w, w_up, w_down, *cast)
    return outs[0], outs[1:]


def _rotary(x, cos, sin_signed):
    lane = lax.broadcasted_iota(jnp.int32, x.shape, 1)
    first_half = (lane % RET_QK) < (RET_QK // 2)
    width = x.shape[1]
    partner = jnp.where(first_half, pltpu.roll(x, width - RET_QK // 2, 1), pltpu.roll(x, RET_QK // 2, 1))
    return x * cos + partner * sin_signed


def _proj_kernel(*refs, transposed, has_state):
    (x_ref, sh_ref, sc_ref, nw_ref, win_ref, cos_ref, sin_ref,
     dec_ref, qd_ref, kd_ref, sd_ref, rn_ref), refs = refs[:12], refs[12:]
    if has_state:
        s0_ref, refs = refs[0], refs[1:]
    a_refs, (or_ref, ga_ref, gr_ref, sout_ref, st_ref) = refs[:5], refs[5:]
    x = x_ref[...]
    nb, ln, _ = x.shape
    h = _rms(x, nw_ref[...] * (1.0 + sc_ref[...])) + sh_ref[...]
    hb = h.reshape(nb * ln, D_MODEL).astype(BF16)

    def proj(i):
        return _dot(hb, win_ref[:, _IN_OFF[i]:_IN_OFF[i + 1]])

    def put(ref, val):
        ref[...] = val.reshape(nb, ln, val.shape[-1]).astype(ref.dtype)

    q = proj(0) * (DA_HD ** -0.5 * LOG2_E)
    k = proj(1)
    v = proj(2)
    if transposed:
        qt_ref, kb_ref, kt_ref, vt_ref, v4_ref = a_refs
        qt_ref[0] = q.T.astype(BF16)
        kt_ref[0] = k.T
        vt_ref[0] = v.T.astype(BF16)
        for hd in range(DA_HEADS):
            kb_ref[0, hd] = k[:, hd * HEAD_W:(hd + 1) * HEAD_W].astype(BF16)
            v4_ref[0, pl.ds(hd, ln, stride=DA_HEADS), :] = v[:, hd * HEAD_W:(hd + 1) * HEAD_W]
    else:
        q_ref, k_ref, kb_ref, v_ref, vb_ref = a_refs
        put(q_ref, q)
        for n in range(nb):
            for hd in range(DA_HEADS):
                for c in range(2):
                    lo = hd * HEAD_W + c * DA_HD
                    k_ref[n, :, hd, c, :] = k[n * ln:(n + 1) * ln, lo:lo + DA_HD]
        put(kb_ref, k)
        for n in range(nb):
            for hd in range(DA_HEADS):
                v_ref[n, pl.ds(hd, ln, stride=DA_HEADS), :] = v[n * ln:(n + 1) * ln, hd * HEAD_W:(hd + 1) * HEAD_W]
        put(vb_ref, v)
    put(ga_ref, proj(7) * 0.5)
    put(gr_ref, proj(8) * 0.5)

    seq_step = pl.program_id(1)

    @pl.when(seq_step == 0)
    def _():
        st_ref[...] = jnp.zeros(st_ref.shape, F32)
        if has_state:
            for n in range(nb):
                for hd in range(RET_HEADS):
                    st_ref[n, hd * RET_QK:(hd + 1) * RET_QK, hd * RET_VD:(hd + 1) * RET_VD] = s0_ref[n, hd]

    cos, sin_signed = cos_ref[...], sin_ref[...]
    q_r = _rotary(proj(3), cos, sin_signed).astype(BF16)
    k_r = _rotary(proj(4), cos, sin_signed) * (RET_QK ** -0.5)
    v_r = proj(5).astype(BF16)
    z_r = proj(6)
    rl = dec_ref.shape[-1]
    for n in range(nb):
        for blk in range(ln // rl):
            rows = slice(n * ln + blk * rl, n * ln + (blk + 1) * rl)
            o, st_ref[n] = _retention_block(q_r[rows], k_r[rows], v_r[rows], st_ref[n],
                                            dec_ref, qd_ref, kd_ref, sd_ref)
            o = o * rn_ref[...] * _silu(z_r[rows])
            or_ref[n, blk * rl:(blk + 1) * rl, :] = o.astype(or_ref.dtype)

    @pl.when(seq_step == pl.num_programs(1) - 1)
    def _():
        for n in range(nb):
            for hd in range(RET_HEADS):
                sout_ref[n, hd] = st_ref[n, hd * RET_QK:(hd + 1) * RET_QK, hd * RET_VD:(hd + 1) * RET_VD]


def _rope_tables(pos, nb):
    half = RET_QK // 2
    inv = ROPE_BASE ** (-np.arange(half, dtype=np.float64) / half)
    ang = np.asarray(pos, np.float64)[:, None] * inv[None, :]
    cos, sin = np.cos(ang), np.sin(ang)
    cos = np.tile(np.concatenate([cos, cos], axis=-1), (nb, RET_HEADS))
    sin = np.tile(np.concatenate([-sin, sin], axis=-1), (nb, RET_HEADS))
    return jnp.asarray(cos, F32), jnp.asarray(sin, F32)


def _proj(x, shift, scale, norm_w, w_in, pos, ret_norm, state, transposed):
    bsz, seq, _ = x.shape
    nb, ln, grid = _token_grid(x)
    cos, sin = _rope_tables(pos, nb)
    tab_spec = pl.BlockSpec((nb * ln, RET_QK_W), lambda b, s: (s, 0))
    rl = min(ln, RET_BLOCK)
    assert ln % rl == 0
    tables = _retention_tables(rl)
    has_state = state is not None
    state_spec = pl.BlockSpec((nb, RET_HEADS, RET_QK, RET_VD), lambda b, s: (b, 0, 0, 0))
    state_shape = jax.ShapeDtypeStruct((bsz, RET_HEADS, RET_QK, RET_VD), F32)
    if transposed:
        assert nb == 1
        t_spec = pl.BlockSpec((1, DA_W, ln), lambda b, s: (b, 0, s))
        t_shape = (bsz, DA_W, seq)
        a_specs = [t_spec, pl.BlockSpec((1, DA_HEADS, ln, HEAD_W), lambda b, s: (b, 0, s, 0)), t_spec, t_spec,
                   pl.BlockSpec((1, DA_HEADS * ln, HEAD_W), lambda b, s: (b, s, 0))]
        a_shapes = [jax.ShapeDtypeStruct(t_shape, BF16), jax.ShapeDtypeStruct((bsz, DA_HEADS, seq, HEAD_W), BF16),
                    jax.ShapeDtypeStruct(t_shape, F32), jax.ShapeDtypeStruct(t_shape, BF16),
                    jax.ShapeDtypeStruct((bsz, DA_HEADS * seq, HEAD_W), F32)]
    else:
        rows = _tok_spec(nb, ln, DA_W)
        a_specs = [rows, pl.BlockSpec((nb, ln, DA_HEADS, 2, DA_HD), lambda b, s: (b, s, 0, 0, 0)), rows,
                   pl.BlockSpec((nb, DA_HEADS * ln, HEAD_W), lambda b, s: (b, s, 0)), rows]
        a_shapes = [jax.ShapeDtypeStruct((bsz, seq, DA_W), BF16),
                    jax.ShapeDtypeStruct((bsz, seq, DA_HEADS, 2, DA_HD), F32),
                    jax.ShapeDtypeStruct((bsz, seq, DA_W), BF16),
                    jax.ShapeDtypeStruct((bsz, DA_HEADS * seq, HEAD_W), F32),
                    jax.ShapeDtypeStruct((bsz, seq, DA_W), BF16)]
    widths = (RET_V_W, D_MODEL, D_MODEL)
    in_specs = [_tok_spec(nb, ln, D_MODEL), _seq_spec(nb), _seq_spec(nb),
                _const_spec((1, D_MODEL)), _const_spec(w_in.shape), tab_spec, tab_spec]
    in_specs += [_const_spec(tab.shape) for tab in tables] + [_const_spec((1, RET_V_W))]
    args = [x, shift, scale, norm_w, w_in, cos, sin, *tables, ret_norm]
    if has_state:
        in_specs.append(state_spec)
        args.append(state)
    return pl.pallas_call(
        functools.partial(_proj_kernel, transposed=transposed, has_state=has_state),
        grid=grid,
        in_specs=in_specs,
        out_specs=a_specs + [_tok_spec(nb, ln, w) for w in widths] + [state_spec],
        out_shape=a_shapes + [jax.ShapeDtypeStruct((bsz, seq, w), BF16) for w in widths] + [state_shape],
        scratch_shapes=[pltpu.VMEM((nb, RET_QK_W, RET_V_W), F32)],
        compiler_params=_params("parallel", "arbitrary"),
        name="proj_t" if transposed else "proj",
    )(*args)


def _lambda(lq1_ref, lk1_ref, lq2_ref, lk2_ref):
    s1 = jnp.sum(lq1_ref[...] * lk1_ref[...], axis=-1, keepdims=True)
    s2 = jnp.sum(lq2_ref[...] * lk2_ref[...], axis=-1, keepdims=True)
    return jnp.exp(s1) - jnp.exp(s2) + LAM_INIT


def _stack_maps(q):
    lane = lax.broadcasted_iota(jnp.int32, q.shape, 1)
    zero = jnp.zeros_like(q)
    return jnp.concatenate([jnp.where(lane < DA_HD, q, zero), jnp.where(lane >= DA_HD, q, zero)], axis=0)


def _prompt_attention(lam, nw_ref, qt_ref, k_ref, vt_ref, o_ref):
    heads, seq = k_ref.shape[1], k_ref.shape[2]
    t = ATTN_T
    feat = lax.broadcasted_iota(jnp.int32, (HEAD_W, t), 0)
    key_chunk = lax.broadcasted_iota(jnp.int32, (t, t), 0) // CHUNK
    query_chunk = lax.broadcasted_iota(jnp.int32, (t, t), 1) // CHUNK
    visible = key_chunk <= query_chunk
    visible2 = jnp.concatenate([visible, visible], axis=1)

    def head_cols(h):
        return slice(h * HEAD_W, (h + 1) * HEAD_W)

    def query_maps(i, h):
        qt = qt_ref[0, head_cols(h), i * t:(i + 1) * t]
        zero = jnp.zeros_like(qt)
        return jnp.concatenate([jnp.where(feat < DA_HD, qt, zero), jnp.where(feat >= DA_HD, qt, zero)], axis=1)

    def scores(i, h):
        lo, hi = i * t, (i + 1) * t
        qm = query_maps(i, h)
        s_diag = jnp.where(visible2, _dot(k_ref[0, h, lo:hi, :], qm), NEG_INF)
        m = jnp.max(s_diag, axis=0, keepdims=True)
        s_full = None
        if i > 0:
            s_full = _dot(k_ref[0, h, :lo, :], qm)
            m = jnp.maximum(m, jnp.max(s_full, axis=0, keepdims=True))
        return s_diag, s_full, m

    ones = jnp.ones((ONES_ROWS, seq), BF16)
    vt1 = [jnp.concatenate([vt_ref[0, head_cols(h), :], ones], axis=0) for h in range(heads)]

    def finish(i, h, s_diag, s_full, m):
        lo, hi = i * t, (i + 1) * t
        acc = _dot(vt1[h][:, lo:hi], jnp.exp2(s_diag - m).astype(BF16))
        if i > 0:
            acc = acc + _dot(vt1[h][:, :lo], jnp.exp2(s_full - m).astype(BF16))
        o2 = acc[:HEAD_W] / acc[HEAD_W:HEAD_W + 1]
        ot = o2[:, :t] - lam * o2[:, t:]
        o_ref[0, h, lo:hi, :] = (_rms(ot.T, nw_ref[h]) * (1.0 - LAM_INIT)).astype(o_ref.dtype)

    n_blocks = seq // t
    pending = [scores(0, h) for h in range(heads)]
    yield
    for i in range(1, n_blocks):
        upcoming = [scores(i, h) for h in range(heads)]
        for h in range(heads):
            finish(i - 1, h, *pending[h])
        pending = upcoming
        yield
    for h in range(heads):
        finish(n_blocks - 1, h, *pending[h])


def _sample_attention(lam, nw_ref, heads, q_ref, kn_ref, vn_ref, kc_ref, vc_ref, o_ref):
    ln = q_ref.shape[1]
    past = kc_ref.shape[-1]
    ones = jnp.ones((past, HEAD_W), BF16)

    def head_cols(j):
        return slice(j * HEAD_W, (j + 1) * HEAD_W)

    def scores(j):
        qs = _stack_maps(q_ref[0, :, head_cols(j)])
        s_cache = _dot(qs, kc_ref[0, j].astype(BF16))
        s_new = _dot_nt(qs, kn_ref[0, :, head_cols(j)])
        m = jnp.maximum(jnp.max(s_cache, axis=-1, keepdims=True), jnp.max(s_new, axis=-1, keepdims=True))
        return s_cache, s_new, m

    def finish(j, s_cache, s_new, m):
        v_cache = vc_ref[0, pl.ds(heads[j], past, stride=DA_HEADS), :].astype(BF16)
        acc = _dot(jnp.exp2(s_cache - m).astype(BF16), jnp.concatenate([v_cache, ones], axis=1))
        v_new = jnp.concatenate([vn_ref[0, :, head_cols(j)], ones[:ln]], axis=1)
        acc = acc + _dot(jnp.exp2(s_new - m).astype(BF16), v_new)
        o2 = acc[:, :HEAD_W] / acc[:, HEAD_W:HEAD_W + 1]
        o = o2[:ln] - lam * o2[ln:]
        o_ref[0, j] = (_rms(o, nw_ref[j]) * (1.0 - LAM_INIT)).astype(o_ref.dtype)

    pending = scores(0)
    yield
    for j in range(1, len(heads)):
        upcoming = scores(j)
        yield
        finish(j - 1, *pending)
        yield
        pending = upcoming
    finish(len(heads) - 1, *pending)


def _weave(main, side, main_per_side=2):
    done = object()
    main_live = side_live = True
    while main_live or side_live:
        if side_live:
            side_live = next(side, done) is not done
        for _ in range(main_per_side):
            if main_live:
                main_live = next(main, done) is not done


def _attn_kernel(lq1_ref, lk1_ref, lq2_ref, lk2_ref, nwp_ref, nws_ref, qt_ref, k_ref, vt_ref,
                 q_ref, kn_ref, vn_ref, kc_ref, vc_ref, op_ref, os_ref):
    lam = _lambda(lq1_ref, lk1_ref, lq2_ref, lk2_ref)
    g = SAMPLE_HEADS_PER_STEP
    first_head = (pl.program_id(0) % (DA_HEADS // g)) * g
    heads = tuple(first_head + j for j in range(g))
    _weave(_prompt_attention(lam, nwp_ref, qt_ref, k_ref, vt_ref, op_ref),
           _sample_attention(lam, nws_ref, heads, q_ref, kn_ref, vn_ref, kc_ref, vc_ref, os_ref))


def _attention(qt, k, vt, q, k_new, v_new, cache_kt, cache_v4, lambdas, da_norm):
    bp, _, seq, _ = k.shape
    bs, ln, _ = q.shape
    past = cache_kt.shape[-1]
    g = SAMPLE_HEADS_PER_STEP
    groups = DA_HEADS // g
    steps = bp * DA_HEADS
    assert steps == bs * groups, "one prompt (sequence, head) per sample (sequence, head group)"
    assert seq % ATTN_T == 0 and ATTN_T % CHUNK == 0 and past % CHUNK == 0 and ln <= CHUNK
    lam_spec = _const_spec((1, DA_HD))
    t_spec = pl.BlockSpec((1, HEAD_W, seq), lambda i: (i // DA_HEADS, i % DA_HEADS, 0))
    head_spec = pl.BlockSpec((1, 1, seq, HEAD_W), lambda i: (i // DA_HEADS, i % DA_HEADS, 0, 0))
    new_spec = pl.BlockSpec((1, ln, g * HEAD_W), lambda i: (i // groups, 0, i % groups))
    out_new_spec = pl.BlockSpec((1, g, ln, HEAD_W), lambda i: (i // groups, i % groups, 0, 0))
    return pl.pallas_call(
        _attn_kernel,
        grid=(steps,),
        in_specs=[lam_spec] * 4 + [pl.BlockSpec((1, 1, HEAD_W), lambda i: (i % DA_HEADS, 0, 0)),
                                   pl.BlockSpec((g, 1, HEAD_W), lambda i: (i % groups, 0, 0)), t_spec, head_spec, t_spec,
                                   new_spec, new_spec, new_spec,
                                   pl.BlockSpec((1, g, HEAD_W, past), lambda i: (i // groups, i % groups, 0, 0)),
                                   pl.BlockSpec((1, DA_HEADS * past, HEAD_W), lambda i: (i // groups, 0, 0))],
        out_specs=[head_spec, out_new_spec],
        out_shape=[jax.ShapeDtypeStruct((bp, DA_HEADS, seq, HEAD_W), BF16),
                   jax.ShapeDtypeStruct((bs, DA_HEADS, ln, HEAD_W), BF16)],
        compiler_params=_params("parallel"),
        name="attn",
    )(*lambdas, da_norm, da_norm, qt, k, vt, q, k_new, v_new, cache_kt, cache_v4)


def _retention_block(q, k_f32, v, state, dec_ref, qd_ref, kd_ref, sd_ref):
    k = k_f32.astype(BF16)
    cross = _dot(q, state.astype(BF16)) * qd_ref[...]
    lane = lax.broadcasted_iota(jnp.int32, q.shape, 1)
    outs = []
    for h in range(RET_HEADS):
        qh = jnp.where((lane >= h * RET_QK) & (lane < (h + 1) * RET_QK), q, jnp.zeros_like(q))
        scores = _dot_nt(qh, k) * dec_ref[h]
        cols = slice(h * RET_VD, (h + 1) * RET_VD)
        oh = _dot(scores.astype(BF16), v[:, cols]) + cross[:, cols]
        xc = oh - jnp.mean(oh, axis=-1, keepdims=True)
        outs.append(xc * lax.rsqrt(jnp.mean(xc * xc, axis=-1, keepdims=True) + EPS))
    kd = (k_f32 * kd_ref[...]).astype(BF16)
    upd = _dot_tn(kd, v)
    r = lax.broadcasted_iota(jnp.int32, upd.shape, 0) // RET_QK
    cl = lax.broadcasted_iota(jnp.int32, upd.shape, 1) // RET_VD
    new_state = state * sd_ref[...] + jnp.where(r == cl, upd, 0.0)
    return jnp.concatenate(outs, axis=-1), new_state


def _retention_tables(ln):
    log_g = np.log(1.0 - 2.0 ** (-5.0 - np.arange(RET_HEADS, dtype=np.float64)))
    idx = np.arange(ln, dtype=np.float64)
    dist = idx[:, None] - idx[None, :]
    decay = np.where(dist >= 0, np.exp(np.maximum(dist, 0.0)[None] * log_g[:, None, None]), 0.0)
    q_decay = np.exp((idx + 1.0)[:, None] * log_g[None, :])
    k_decay = np.exp((ln - 1.0 - idx)[:, None] * log_g[None, :])
    s_decay = np.exp(ln * log_g)[None, :]
    tables = (decay, np.repeat(q_decay, RET_VD, axis=1), np.repeat(k_decay, RET_QK, axis=1),
              np.repeat(s_decay, RET_VD, axis=1))
    return tuple(jnp.asarray(tab, F32) for tab in tables)


def _merge_ffn_kernel(x_ref, gm_ref, oa_ref, or_ref, ga_ref, gr_ref, wa_ref, wr_ref, wo_ref,
                      sh_ref, sc_ref, g_ref, nw_ref, wup_ref, wdn_ref, nf_ref, o_ref):
    x = x_ref[...]
    nb, ln, _ = x.shape

    def flat(ref):
        return ref[...].reshape(nb * ln, ref.shape[-1])

    o_a = jnp.concatenate([oa_ref[:, hd] for hd in range(DA_HEADS)], axis=-1)
    a = _dot(o_a.reshape(nb * ln, DA_W), wa_ref[...])
    r = _dot(flat(or_ref), wr_ref[...])
    merged2 = (1.0 + jnp.tanh(flat(ga_ref).astype(F32))) * a + (1.0 + jnp.tanh(flat(gr_ref).astype(F32))) * r
    y2 = _dot(merged2.astype(BF16), wo_ref[...]).reshape(nb, ln, D_MODEL)
    x = x + (0.5 * gm_ref[...]) * y2
    x = _half_step(x, sh_ref, sc_ref, g_ref, nw_ref, wup_ref, wdn_ref)
    o_ref[...] = _rms(x, nf_ref[...])


def _merge_ffn(x, gate_m, o_a, o_r, gate_a, gate_r, w_a, w_r, w_o, shift, scale, gate, norm_w, w_up, w_down,
               norm_final):
    nb, ln, grid = _token_grid(x)
    return pl.pallas_call(
        _merge_ffn_kernel,
        grid=grid,
        in_specs=[_tok_spec(nb, ln, D_MODEL), _seq_spec(nb),
                  pl.BlockSpec((nb, DA_HEADS, ln, HEAD_W), lambda b, s: (b, 0, s, 0)), _tok_spec(nb, ln, RET_V_W),
                  _tok_spec(nb, ln, D_MODEL), _tok_spec(nb, ln, D_MODEL),
                  _const_spec(w_a.shape), _const_spec(w_r.shape), _const_spec(w_o.shape),
                  _seq_spec(nb), _seq_spec(nb), _seq_spec(nb),
                  _const_spec((1, D_MODEL)), _const_spec(w_up.shape), _const_spec(w_down.shape),
                  _const_spec((1, D_MODEL))],
        out_specs=_tok_spec(nb, ln, D_MODEL),
        out_shape=jax.ShapeDtypeStruct(x.shape, F32),
        compiler_params=_params("parallel", "parallel"),
        name="merge_ffn",
    )(x, gate_m, o_a, o_r, gate_a, gate_r, w_a, w_r, w_o, shift, scale, gate, norm_w, w_up, w_down, norm_final)


def _first_half_step(x, mod, w, cast, tile):
    sh1, sc1, g1 = [mod[:, i:i + 1] for i in range(3)]
    return _ffn(x, sh1, sc1, g1, w["norm_f1"], w["w_up1"], w["w_down1"], cast, tile)


def _mixer_inputs(x, mod, w, state, offset):
    shm, scm = [mod[:, i:i + 1] for i in range(3, 5)]
    pos = offset + np.arange(x.shape[1])
    return _proj(x, shm, scm, w["norm_mix"], w["w_in"], pos, w["ret_norm"], state, transposed=state is None)


def _after_attention(x, mod, w, norm_final, o_a, o_r, gate_a, gate_r):
    gm, sh2, sc2, g2 = [mod[:, i:i + 1] for i in range(5, N_MOD)]
    return _merge_ffn(x, gm, o_a, o_r, gate_a, gate_r, w["w_a_proj"], w["w_r_proj"], w["w_o"],
                      sh2, sc2, g2, w["norm_f2"], w["w_up2"], w["w_down2"], norm_final)


def kernel(x_prompt, x_sample, c_prompt, c_sample, cache_k, cache_v, state_ret, norm_f1, w_up1, w_down1, norm_mix, w_in, lambda_q1, lambda_k1, lambda_q2, lambda_k2, da_norm, ret_norm, w_a_proj, w_r_proj, w_o, norm_f2, w_up2, w_down2, w_ada, b_ada, norm_final):
    assert cache_k.shape[0] == 1, "single layer"
    n_prompt = c_prompt.shape[0]
    mod = _modulation(jnp.concatenate([c_prompt, c_sample], axis=0), w_ada[0], b_ada[0])
    mod = mod.reshape(-1, N_MOD, D_MODEL)
    w = {
        "norm_f1": norm_f1, "norm_mix": norm_mix, "norm_f2": norm_f2,
        "w_up1": w_up1[0].astype(BF16), "w_down1": w_down1[0].astype(BF16),
        "lambdas": (lambda_q1, lambda_k1, lambda_q2, lambda_k2),
        "ret_norm": ret_norm[0].reshape(1, RET_V_W),
    }
    nf = norm_final.reshape(1, D_MODEL)
    dec_b, past_len = cache_k.shape[1], cache_k.shape[2]
    cache_kt = jnp.transpose(cache_k[0], (0, 2, 3, 4, 1)).reshape(dec_b, DA_HEADS, HEAD_W, past_len)
    cache_v4 = cache_v[0].reshape(dec_b, past_len * DA_HEADS, HEAD_W)
    mod_p, mod_s = mod[:n_prompt], mod[n_prompt:]
    bp, seq = x_prompt.shape[:2]
    ln = x_sample.shape[1]

    x_s, (w["w_in"],) = _first_half_step(x_sample, mod_s, w, (w_in[0],), TOKEN_TILE)
    q_s, k_s, kb_s, v_s, vb_s, or_s, ga_s, gr_s, st_s = _mixer_inputs(x_s, mod_s, w, state_ret[0], past_len)
    later = (w_a_proj[0], w_r_proj[0], w_o[0], w_up2[0], w_down2[0])
    x_p, (w["w_a_proj"], w["w_r_proj"], w["w_o"], w["w_up2"], w["w_down2"]) = _first_half_step(
        x_prompt, mod_p, w, later, FFN_TOKEN_TILE)
    qt_p, kb_p, kt_p, vt_p, v4_p, or_p, ga_p, gr_p, st_p = _mixer_inputs(x_p, mod_p, w, None, 0)
    oa_p, oa_s = _attention(qt_p, kb_p, vt_p, q_s, kb_s, vb_s, cache_kt, cache_v4, w["lambdas"],
                            da_norm[0].reshape(DA_HEADS, 1, HEAD_W))
    y_p = _after_attention(x_p, mod_p, w, nf, oa_p, or_p, ga_p, gr_p)
    y_s = _after_attention(x_s, mod_s, w, nf, oa_s, or_s, ga_s, gr_s)

    k_new_p = jnp.transpose(kt_p.reshape(bp, DA_HEADS, 2, DA_HD, seq), (0, 4, 1, 2, 3))[None]
    v_new_p = v4_p.reshape(1, bp, seq, DA_HEADS, HEAD_W)
    k_new_s = k_s[None]
    v_new_s = v_s.reshape(1, dec_b, ln, DA_HEADS, HEAD_W)
    return y_p, y_s, k_new_p, v_new_p, st_p[None], k_new_s, v_new_s, st_s[None]
```

```python
import functools
import math

import jax
import jax.numpy as jnp
import numpy as np
from jax import lax
from jax.experimental import pallas as pl
from jax.experimental.pallas import tpu as pltpu

F32 = jnp.float32
BF16 = jnp.bfloat16

D_MODEL = 1024
D_FF = 2816
N_MOD = 9
CHUNK = 64
DA_HEADS = 4
DA_HD = 64
DA_W = 512
HEAD_W = 128
RET_HEADS = 4
RET_QK = 64
RET_QK_W = 256
RET_VD = 128
RET_V_W = 512
ROPE_BASE = 10000.0
EPS = 1e-6
NEG_INF = -1e30
LAM_INIT = 0.8 - 0.6 * math.exp(-0.3 * 0)
LOG2_E = math.log2(math.e)
ONES_ROWS = 16

_IN_SIZES = (DA_W, DA_W, DA_W, RET_QK_W, RET_QK_W, RET_V_W, RET_V_W, D_MODEL, D_MODEL)
_IN_OFF = [0]
for _s in _IN_SIZES:
    _IN_OFF.append(_IN_OFF[-1] + _s)

TOKEN_TILE = 512
FFN_TOKEN_TILE = 1024
BF16_ROW_TILE = 16
ATTN_T = 256
SAMPLE_HEADS_PER_STEP = 2
RET_BLOCK = 256
V7X_VMEM_BYTES = 64 * 1024 * 1024
VMEM_LIMIT = V7X_VMEM_BYTES * 7 // 8


def _const_spec(shape):
    nd = len(shape)
    return pl.BlockSpec(shape, lambda *_: (0,) * nd, pipeline_mode=pl.Buffered(1))


def _params(*sem):
    return pltpu.CompilerParams(dimension_semantics=sem, vmem_limit_bytes=VMEM_LIMIT)


def _rms(x, w):
    ms = jnp.mean(x * x, axis=-1, keepdims=True)
    return x * lax.rsqrt(ms + EPS) * w


def _sigmoid(x):
    return 0.5 * (1.0 + jnp.tanh(0.5 * x))


def _silu(x):
    return x * _sigmoid(x)


def _dot(a, b):
    return jnp.dot(a, b, preferred_element_type=F32)


def _dot_nt(a, b):
    return lax.dot_general(a, b, (((1,), (1,)), ((), ())), preferred_element_type=F32)


def _dot_tn(a, b):
    return lax.dot_general(a, b, (((0,), (0,)), ((), ())), preferred_element_type=F32)


def _mod_kernel(c_ref, w_ref, b_ref, o_ref):
    s = _silu(c_ref[...]).astype(BF16)
    o_ref[...] = _dot(s, w_ref[...].astype(BF16)) + b_ref[...]


def _modulation(c, w_ada, b_ada):
    n, width = c.shape[0], w_ada.shape[1]
    tn = width // 8
    return pl.pallas_call(
        _mod_kernel,
        grid=(width // tn,),
        in_specs=[_const_spec((n, D_MODEL)),
                  pl.BlockSpec((D_MODEL, tn), lambda j: (0, j)),
                  pl.BlockSpec((1, tn), lambda j: (0, j))],
        out_specs=pl.BlockSpec((n, tn), lambda j: (0, j)),
        out_shape=jax.ShapeDtypeStruct((n, width), F32),
        compiler_params=_params("parallel"),
        name="mod",
    )(c, w_ada, b_ada.reshape(1, width))


def _token_grid(x, tile=TOKEN_TILE):
    bsz, seq, _ = x.shape
    ln = min(seq, tile)
    nb = tile // ln
    assert seq % ln == 0 and bsz % nb == 0
    return nb, ln, (bsz // nb, seq // ln)


def _tok_spec(nb, ln, width):
    return pl.BlockSpec((nb, ln, width), lambda b, s: (b, s, 0))


def _seq_spec(nb):
    return pl.BlockSpec((nb, 1, D_MODEL), lambda b, s: (b, 0, 0))


def _half_step(x, sh_ref, sc_ref, g_ref, nw_ref, wup_ref, wdn_ref):
    nb, ln, _ = x.shape
    h = _rms(x, nw_ref[...] * (1.0 + sc_ref[...])) + sh_ref[...]
    hb = h.reshape(nb * ln, D_MODEL).astype(BF16)
    a = _dot(hb, wup_ref[:, :D_FF])
    b = _dot(hb, wup_ref[:, D_FF:])
    act2 = (a * b * (1.0 + jnp.tanh(0.5 * a))).astype(BF16)
    y2 = _dot(act2, wdn_ref[...]).reshape(nb, ln, D_MODEL)
    return x + (0.25 * g_ref[...]) * y2


def _ffn_kernel(*refs, cast_steps, grid_steps):
    n = len(cast_steps)
    x_ref, sh_ref, sc_ref, g_ref, nw_ref, wup_ref, wdn_ref = refs[:7]
    src_refs, o_ref, dst_refs = refs[7:7 + n], refs[7 + n], refs[8 + n:]
    o_ref[...] = _half_step(x_ref[...], sh_ref, sc_ref, g_ref, nw_ref, wup_ref, wdn_ref)
    step = pl.program_id(0) * pl.num_programs(1) + pl.program_id(1)
    for src_ref, dst_ref, steps in zip(src_refs, dst_refs, cast_steps):
        def cast_chunk(src_ref=src_ref, dst_ref=dst_ref):
            dst_ref[...] = src_ref[...].astype(BF16)
        if steps == grid_steps:
            cast_chunk()
        else:
            pl.when(step < steps)(cast_chunk)


def _cast_chunks(rows, n_steps):
    return max(d for d in range(1, n_steps + 1) if rows % d == 0 and (rows // d) % BF16_ROW_TILE == 0)


def _ffn(x, shift, scale, gate, norm_w, w_up, w_down, cast=(), tile=TOKEN_TILE):
    nb, ln, grid = _token_grid(x, tile)
    gs = grid[1]
    cast_steps = tuple(_cast_chunks(w.shape[0], grid[0] * gs) for w in cast)

    def chunk_spec(w, steps):
        return pl.BlockSpec((w.shape[0] // steps, w.shape[1]), lambda b, s: (jnp.minimum(b * gs + s, steps - 1), 0))

    chunk_specs = [chunk_spec(w, steps) for w, steps in zip(cast, cast_steps)]
    outs = pl.pallas_call(
        functools.partial(_ffn_kernel, cast_steps=cast_steps, grid_steps=grid[0] * gs),
        grid=grid,
        in_specs=[_tok_spec(nb, ln, D_MODEL), _seq_spec(nb), _seq_spec(nb), _seq_spec(nb),
                  _const_spec((1, D_MODEL)), _const_spec(w_up.shape), _const_spec(w_down.shape)] + chunk_specs,
        out_specs=[_tok_spec(nb, ln, D_MODEL)] + chunk_specs,
        out_shape=[jax.ShapeDtypeStruct(x.shape, F32)] + [jax.ShapeDtypeStruct(w.shape, BF16) for w in cast],
        compiler_params=_params("arbitrary", "arbitrary"),
        name="ffn",
    )(x, shift, scale, gate, norm_w, w_up, w_down, *cast)
    return outs[0], outs[1:]


def _rotary(x, cos, sin_signed):
    lane = lax.broadcasted_iota(jnp.int32, x.shape, 1)
    first_half = (lane % RET_QK) < (RET_QK // 2)
    width = x.shape[1]
    partner = jnp.where(first_half, pltpu.roll(x, width - RET_QK // 2, 1), pltpu.roll(x, RET_QK // 2, 1))
    return x * cos + partner * sin_signed


def _proj_kernel(*refs, transposed, has_state):
    (x_ref, sh_ref, sc_ref, nw_ref, win_ref, cos_ref, sin_ref,
     dec_ref, qd_ref, kd_ref, sd_ref, rn_ref), refs = refs[:12], refs[12:]
    if has_state:
        s0_ref, refs = refs[0], refs[1:]
    a_refs, (or_ref, ga_ref, gr_ref, sout_ref, st_ref) = refs[:5], refs[5:]
    x = x_ref[...]
    nb, ln, _ = x.shape
    h = _rms(x, nw_ref[...] * (1.0 + sc_ref[...])) + sh_ref[...]
    hb = h.reshape(nb * ln, D_MODEL).astype(BF16)

    def proj(i):
        return _dot(hb, win_ref[:, _IN_OFF[i]:_IN_OFF[i + 1]])

    def put(ref, val):
        ref[...] = val.reshape(nb, ln, val.shape[-1]).astype(ref.dtype)

    q = proj(0) * (DA_HD ** -0.5 * LOG2_E)
    k = proj(1)
    v = proj(2)
    if transposed:
        qt_ref, kb_ref, kt_ref, vt_ref, v4_ref = a_refs
        qt_ref[0] = q.T.astype(BF16)
        kt_ref[0] = k.T
        vt_ref[0] = v.T.astype(BF16)
        for hd in range(DA_HEADS):
            kb_ref[0, hd] = k[:, hd * HEAD_W:(hd + 1) * HEAD_W].astype(BF16)
            v4_ref[0, pl.ds(hd, ln, stride=DA_HEADS), :] = v[:, hd * HEAD_W:(hd + 1) * HEAD_W]
    else:
        q_ref, k_ref, kb_ref, v_ref, vb_ref = a_refs
        put(q_ref, q)
        for n in range(nb):
            for hd in range(DA_HEADS):
                for c in range(2):
                    lo = hd * HEAD_W + c * DA_HD
                    k_ref[n, :, hd, c, :] = k[n * ln:(n + 1) * ln, lo:lo + DA_HD]
        put(kb_ref, k)
        for n in range(nb):
            for hd in range(DA_HEADS):
                v_ref[n, pl.ds(hd, ln, stride=DA_HEADS), :] = v[n * ln:(n + 1) * ln, hd * HEAD_W:(hd + 1) * HEAD_W]
        put(vb_ref, v)
    put(ga_ref, proj(7) * 0.5)
    put(gr_ref, proj(8) * 0.5)

    seq_step = pl.program_id(1)

    @pl.when(seq_step == 0)
    def _():
        st_ref[...] = jnp.zeros(st_ref.shape, F32)
        if has_state:
            for n in range(nb):
                for hd in range(RET_HEADS):
                    st_ref[n, hd * RET_QK:(hd + 1) * RET_QK, hd * RET_VD:(hd + 1) * RET_VD] = s0_ref[n, hd]

    cos, sin_signed = cos_ref[...], sin_ref[...]
    q_r = _rotary(proj(3), cos, sin_signed).astype(BF16)
    k_r = _rotary(proj(4), cos, sin_signed) * (RET_QK ** -0.5)
    v_r = proj(5).astype(BF16)
    z_r = proj(6)
    rl = dec_ref.shape[-1]
    for n in range(nb):
        for blk in range(ln // rl):
            rows = slice(n * ln + blk * rl, n * ln + (blk + 1) * rl)
            o, st_ref[n] = _retention_block(q_r[rows], k_r[rows], v_r[rows], st_ref[n],
                                            dec_ref, qd_ref, kd_ref, sd_ref)
            o = o * rn_ref[...] * _silu(z_r[rows])
            or_ref[n, blk * rl:(blk + 1) * rl, :] = o.astype(or_ref.dtype)

    @pl.when(seq_step == pl.num_programs(1) - 1)
    def _():
        for n in range(nb):
            for hd in range(RET_HEADS):
                sout_ref[n, hd] = st_ref[n, hd * RET_QK:(hd + 1) * RET_QK, hd * RET_VD:(hd + 1) * RET_VD]


def _rope_tables(pos, nb):
    half = RET_QK // 2
    inv = ROPE_BASE ** (-np.arange(half, dtype=np.float64) / half)
    ang = np.asarray(pos, np.float64)[:, None] * inv[None, :]
    cos, sin = np.cos(ang), np.sin(ang)
    cos = np.tile(np.concatenate([cos, cos], axis=-1), (nb, RET_HEADS))
    sin = np.tile(np.concatenate([-sin, sin], axis=-1), (nb, RET_HEADS))
    return jnp.asarray(cos, F32), jnp.asarray(sin, F32)


def _proj(x, shift, scale, norm_w, w_in, pos, ret_norm, state, transposed):
    bsz, seq, _ = x.shape
    nb, ln, grid = _token_grid(x)
    cos, sin = _rope_tables(pos, nb)
    tab_spec = pl.BlockSpec((nb * ln, RET_QK_W), lambda b, s: (s, 0))
    rl = min(ln, RET_BLOCK)
    assert ln % rl == 0
    tables = _retention_tables(rl)
    has_state = state is not None
    state_spec = pl.BlockSpec((nb, RET_HEADS, RET_QK, RET_VD), lambda b, s: (b, 0, 0, 0))
    state_shape = jax.ShapeDtypeStruct((bsz, RET_HEADS, RET_QK, RET_VD), F32)
    if transposed:
        assert nb == 1
        t_spec = pl.BlockSpec((1, DA_W, ln), lambda b, s: (b, 0, s))
        t_shape = (bsz, DA_W, seq)
        a_specs = [t_spec, pl.BlockSpec((1, DA_HEADS, ln, HEAD_W), lambda b, s: (b, 0, s, 0)), t_spec, t_spec,
                   pl.BlockSpec((1, DA_HEADS * ln, HEAD_W), lambda b, s: (b, s, 0))]
        a_shapes = [jax.ShapeDtypeStruct(t_shape, BF16), jax.ShapeDtypeStruct((bsz, DA_HEADS, seq, HEAD_W), BF16),
                    jax.ShapeDtypeStruct(t_shape, F32), jax.ShapeDtypeStruct(t_shape, BF16),
                    jax.ShapeDtypeStruct((bsz, DA_HEADS * seq, HEAD_W), F32)]
    else:
        rows = _tok_spec(nb, ln, DA_W)
        a_specs = [rows, pl.BlockSpec((nb, ln, DA_HEADS, 2, DA_HD), lambda b, s: (b, s, 0, 0, 0)), rows,
                   pl.BlockSpec((nb, DA_HEADS * ln, HEAD_W), lambda b, s: (b, s, 0)), rows]
        a_shapes = [jax.ShapeDtypeStruct((bsz, seq, DA_W), BF16),
                    jax.ShapeDtypeStruct((bsz, seq, DA_HEADS, 2, DA_HD), F32),
                    jax.ShapeDtypeStruct((bsz, seq, DA_W), BF16),
                    jax.ShapeDtypeStruct((bsz, DA_HEADS * seq, HEAD_W), F32),
                    jax.ShapeDtypeStruct((bsz, seq, DA_W), BF16)]
    widths = (RET_V_W, D_MODEL, D_MODEL)
    in_specs = [_tok_spec(nb, ln, D_MODEL), _seq_spec(nb), _seq_spec(nb),
                _const_spec((1, D_MODEL)), _const_spec(w_in.shape), tab_spec, tab_spec]
    in_specs += [_const_spec(tab.shape) for tab in tables] + [_const_spec((1, RET_V_W))]
    args = [x, shift, scale, norm_w, w_in, cos, sin, *tables, ret_norm]
    if has_state:
        in_specs.append(state_spec)
        args.append(state)
    return pl.pallas_call(
        functools.partial(_proj_kernel, transposed=transposed, has_state=has_state),
        grid=grid,
        in_specs=in_specs,
        out_specs=a_specs + [_tok_spec(nb, ln, w) for w in widths] + [state_spec],
        out_shape=a_shapes + [jax.ShapeDtypeStruct((bsz, seq, w), BF16) for w in widths] + [state_shape],
        scratch_shapes=[pltpu.VMEM((nb, RET_QK_W, RET_V_W), F32)],
        compiler_params=_params("parallel", "arbitrary"),
        name="proj_t" if transposed else "proj",
    )(*args)


def _lambda(lq1_ref, lk1_ref, lq2_ref, lk2_ref):
    s1 = jnp.sum(lq1_ref[...] * lk1_ref[...], axis=-1, keepdims=True)
    s2 = jnp.sum(lq2_ref[...] * lk2_ref[...], axis=-1, keepdims=True)
    return jnp.exp(s1) - jnp.exp(s2) + LAM_INIT


def _stack_maps(q):
    lane = lax.broadcasted_iota(jnp.int32, q.shape, 1)
    zero = jnp.zeros_like(q)
    return jnp.concatenate([jnp.where(lane < DA_HD, q, zero), jnp.where(lane >= DA_HD, q, zero)], axis=0)


def _prompt_attention(lam, nw_ref, qt_ref, k_ref, vt_ref, o_ref):
    heads, seq = k_ref.shape[1], k_ref.shape[2]
    t = ATTN_T
    feat = lax.broadcasted_iota(jnp.int32, (HEAD_W, t), 0)
    key_chunk = lax.broadcasted_iota(jnp.int32, (t, t), 0) // CHUNK
    query_chunk = lax.broadcasted_iota(jnp.int32, (t, t), 1) // CHUNK
    visible = key_chunk <= query_chunk
    visible2 = jnp.concatenate([visible, visible], axis=1)

    def head_cols(h):
        return slice(h * HEAD_W, (h + 1) * HEAD_W)

    def query_maps(i, h):
        qt = qt_ref[0, head_cols(h), i * t:(i + 1) * t]
        zero = jnp.zeros_like(qt)
        return jnp.concatenate([jnp.where(feat < DA_HD, qt, zero), jnp.where(feat >= DA_HD, qt, zero)], axis=1)

    def scores(i, h):
        lo, hi = i * t, (i + 1) * t
        qm = query_maps(i, h)
        s_diag = jnp.where(visible2, _dot(k_ref[0, h, lo:hi, :], qm), NEG_INF)
        m = jnp.max(s_diag, axis=0, keepdims=True)
        s_full = None
        if i > 0:
            s_full = _dot(k_ref[0, h, :lo, :], qm)
            m = jnp.maximum(m, jnp.max(s_full, axis=0, keepdims=True))
        return s_diag, s_full, m

    ones = jnp.ones((ONES_ROWS, seq), BF16)
    vt1 = [jnp.concatenate([vt_ref[0, head_cols(h), :], ones], axis=0) for h in range(heads)]

    def finish(i, h, s_diag, s_full, m):
        lo, hi = i * t, (i + 1) * t
        acc = _dot(vt1[h][:, lo:hi], jnp.exp2(s_diag - m).astype(BF16))
        if i > 0:
            acc = acc + _dot(vt1[h][:, :lo], jnp.exp2(s_full - m).astype(BF16))
        o2 = acc[:HEAD_W] / acc[HEAD_W:HEAD_W + 1]
        ot = o2[:, :t] - lam * o2[:, t:]
        o_ref[0, h, lo:hi, :] = (_rms(ot.T, nw_ref[h]) * (1.0 - LAM_INIT)).astype(o_ref.dtype)

    n_blocks = seq // t
    pending = [scores(0, h) for h in range(heads)]
    yield
    for i in range(1, n_blocks):
        upcoming = [scores(i, h) for h in range(heads)]
        for h in range(heads):
            finish(i - 1, h, *pending[h])
        pending = upcoming
        yield
    for h in range(heads):
        finish(n_blocks - 1, h, *pending[h])


def _sample_attention(lam, nw_ref, heads, q_ref, kn_ref, vn_ref, kc_ref, vc_ref, o_ref):
    ln = q_ref.shape[1]
    past = kc_ref.shape[-1]
    ones = jnp.ones((past, HEAD_W), BF16)

    def head_cols(j):
        return slice(j * HEAD_W, (j + 1) * HEAD_W)

    def scores(j):
        qs = _stack_maps(q_ref[0, :, head_cols(j)])
        s_cache = _dot(qs, kc_ref[0, j].astype(BF16))
        s_new = _dot_nt(qs, kn_ref[0, :, head_cols(j)])
        m = jnp.maximum(jnp.max(s_cache, axis=-1, keepdims=True), jnp.max(s_new, axis=-1, keepdims=True))
        return s_cache, s_new, m

    def finish(j, s_cache, s_new, m):
        v_cache = vc_ref[0, pl.ds(heads[j], past, stride=DA_HEADS), :].astype(BF16)
        acc = _dot(jnp.exp2(s_cache - m).astype(BF16), jnp.concatenate([v_cache, ones], axis=1))
        v_new = jnp.concatenate([vn_ref[0, :, head_cols(j)], ones[:ln]], axis=1)
        acc = acc + _dot(jnp.exp2(s_new - m).astype(BF16), v_new)
        o2 = acc[:, :HEAD_W] / acc[:, HEAD_W:HEAD_W + 1]
        o = o2[:ln] - lam * o2[ln:]
        o_ref[0, j] = (_rms(o, nw_ref[j]) * (1.0 - LAM_INIT)).astype(o_ref.dtype)

    pending = scores(0)
    yield
    for j in range(1, len(heads)):
        upcoming = scores(j)
        yield
        finish(j - 1, *pending)
        yield
        pending = upcoming
    finish(len(heads) - 1, *pending)


def _weave(main, side, main_per_side=2):
    done = object()
    main_live = side_live = True
    while main_live or side_live:
        if side_live:
            side_live = next(side, done) is not done
        for _ in range(main_per_side):
            if main_live:
                main_live = next(main, done) is not done


def _attn_kernel(lq1_ref, lk1_ref, lq2_ref, lk2_ref, nwp_ref, nws_ref, qt_ref, k_ref, vt_ref,
                 q_ref, kn_ref, vn_ref, kc_ref, vc_ref, op_ref, os_ref):
    lam = _lambda(lq1_ref, lk1_ref, lq2_ref, lk2_ref)
    g = SAMPLE_HEADS_PER_STEP
    first_head = (pl.program_id(0) % (DA_HEADS // g)) * g
    heads = tuple(first_head + j for j in range(g))
    _weave(_prompt_attention(lam, nwp_ref, qt_ref, k_ref, vt_ref, op_ref),
           _sample_attention(lam, nws_ref, heads, q_ref, kn_ref, vn_ref, kc_ref, vc_ref, os_ref))


def _attention(qt, k, vt, q, k_new, v_new, cache_kt, cache_v4, lambdas, da_norm):
    bp, _, seq, _ = k.shape
    bs, ln, _ = q.shape
    past = cache_kt.shape[-1]
    g = SAMPLE_HEADS_PER_STEP
    groups = DA_HEADS // g
    steps = bp * DA_HEADS
    assert steps == bs * groups, "one prompt (sequence, head) per sample (sequence, head group)"
    assert seq % ATTN_T == 0 and ATTN_T % CHUNK == 0 and past % CHUNK == 0 and ln <= CHUNK
    lam_spec = _const_spec((1, DA_HD))
    t_spec = pl.BlockSpec((1, HEAD_W, seq), lambda i: (i // DA_HEADS, i % DA_HEADS, 0))
    head_spec = pl.BlockSpec((1, 1, seq, HEAD_W), lambda i: (i // DA_HEADS, i % DA_HEADS, 0, 0))
    new_spec = pl.BlockSpec((1, ln, g * HEAD_W), lambda i: (i // groups, 0, i % groups))
    out_new_spec = pl.BlockSpec((1, g, ln, HEAD_W), lambda i: (i // groups, i % groups, 0, 0))
    return pl.pallas_call(
        _attn_kernel,
        grid=(steps,),
        in_specs=[lam_spec] * 4 + [pl.BlockSpec((1, 1, HEAD_W), lambda i: (i % DA_HEADS, 0, 0)),
                                   pl.BlockSpec((g, 1, HEAD_W), lambda i: (i % groups, 0, 0)), t_spec, head_spec, t_spec,
                                   new_spec, new_spec, new_spec,
                                   pl.BlockSpec((1, g, HEAD_W, past), lambda i: (i // groups, i % groups, 0, 0)),
                                   pl.BlockSpec((1, DA_HEADS * past, HEAD_W), lambda i: (i // groups, 0, 0))],
        out_specs=[head_spec, out_new_spec],
        out_shape=[jax.ShapeDtypeStruct((bp, DA_HEADS, seq, HEAD_W), BF16),
                   jax.ShapeDtypeStruct((bs, DA_HEADS, ln, HEAD_W), BF16)],
        compiler_params=_params("parallel"),
        name="attn",
    )(*lambdas, da_norm, da_norm, qt, k, vt, q, k_new, v_new, cache_kt, cache_v4)


def _retention_block(q, k_f32, v, state, dec_ref, qd_ref, kd_ref, sd_ref):
    k = k_f32.astype(BF16)
    cross = _dot(q, state.astype(BF16)) * qd_ref[...]
    lane = lax.broadcasted_iota(jnp.int32, q.shape, 1)
    outs = []
    for h in range(RET_HEADS):
        qh = jnp.where((lane >= h * RET_QK) & (lane < (h + 1) * RET_QK), q, jnp.zeros_like(q))
        scores = _dot_nt(qh, k) * dec_ref[h]
        cols = slice(h * RET_VD, (h + 1) * RET_VD)
        oh = _dot(scores.astype(BF16), v[:, cols]) + cross[:, cols]
        xc = oh - jnp.mean(oh, axis=-1, keepdims=True)
        outs.append(xc * lax.rsqrt(jnp.mean(xc * xc, axis=-1, keepdims=True) + EPS))
    kd = (k_f32 * kd_ref[...]).astype(BF16)
    upd = _dot_tn(kd, v)
    r = lax.broadcasted_iota(jnp.int32, upd.shape, 0) // RET_QK
    cl = lax.broadcasted_iota(jnp.int32, upd.shape, 1) // RET_VD
    new_state = state * sd_ref[...] + jnp.where(r == cl, upd, 0.0)
    return jnp.concatenate(outs, axis=-1), new_state


def _retention_tables(ln):
    log_g = np.log(1.0 - 2.0 ** (-5.0 - np.arange(RET_HEADS, dtype=np.float64)))
    idx = np.arange(ln, dtype=np.float64)
    dist = idx[:, None] - idx[None, :]
    decay = np.where(dist >= 0, np.exp(np.maximum(dist, 0.0)[None] * log_g[:, None, None]), 0.0)
    q_decay = np.exp((idx + 1.0)[:, None] * log_g[None, :])
    k_decay = np.exp((ln - 1.0 - idx)[:, None] * log_g[None, :])
    s_decay = np.exp(ln * log_g)[None, :]
    tables = (decay, np.repeat(q_decay, RET_VD, axis=1), np.repeat(k_decay, RET_QK, axis=1),
              np.repeat(s_decay, RET_VD, axis=1))
    return tuple(jnp.asarray(tab, F32) for tab in tables)


def _merge_ffn_kernel(x_ref, gm_ref, oa_ref, or_ref, ga_ref, gr_ref, wa_ref, wr_ref, wo_ref,
                      sh_ref, sc_ref, g_ref, nw_ref, wup_ref, wdn_ref, nf_ref, o_ref):
    x = x_ref[...]
    nb, ln, _ = x.shape

    def flat(ref):
        return ref[...].reshape(nb * ln, ref.shape[-1])

    o_a = jnp.concatenate([oa_ref[:, hd] for hd in range(DA_HEADS)], axis=-1)
    a = _dot(o_a.reshape(nb * ln, DA_W), wa_ref[...])
    r = _dot(flat(or_ref), wr_ref[...])
    merged2 = (1.0 + jnp.tanh(flat(ga_ref).astype(F32))) * a + (1.0 + jnp.tanh(flat(gr_ref).astype(F32))) * r
    y2 = _dot(merged2.astype(BF16), wo_ref[...]).reshape(nb, ln, D_MODEL)
    x = x + (0.5 * gm_ref[...]) * y2
    x = _half_step(x, sh_ref, sc_ref, g_ref, nw_ref, wup_ref, wdn_ref)
    o_ref[...] = _rms(x, nf_ref[...])


def _merge_ffn(x, gate_m, o_a, o_r, gate_a, gate_r, w_a, w_r, w_o, shift, scale, gate, norm_w, w_up, w_down,
               norm_final):
    nb, ln, grid = _token_grid(x)
    return pl.pallas_call(
        _merge_ffn_kernel,
        grid=grid,
        in_specs=[_tok_spec(nb, ln, D_MODEL), _seq_spec(nb),
                  pl.BlockSpec((nb, DA_HEADS, ln, HEAD_W), lambda b, s: (b, 0, s, 0)), _tok_spec(nb, ln, RET_V_W),
                  _tok_spec(nb, ln, D_MODEL), _tok_spec(nb, ln, D_MODEL),
                  _const_spec(w_a.shape), _const_spec(w_r.shape), _const_spec(w_o.shape),
                  _seq_spec(nb), _seq_spec(nb), _seq_spec(nb),
                  _const_spec((1, D_MODEL)), _const_spec(w_up.shape), _const_spec(w_down.shape),
                  _const_spec((1, D_MODEL))],
        out_specs=_tok_spec(nb, ln, D_MODEL),
        out_shape=jax.ShapeDtypeStruct(x.shape, F32),
        compiler_params=_params("parallel", "parallel"),
        name="merge_ffn",
    )(x, gate_m, o_a, o_r, gate_a, gate_r, w_a, w_r, w_o, shift, scale, gate, norm_w, w_up, w_down, norm_final)


def _first_half_step(x, mod, w, cast, tile):
    sh1, sc1, g1 = [mod[:, i:i + 1] for i in range(3)]
    return _ffn(x, sh1, sc1, g1, w["norm_f1"], w["w_up1"], w["w_down1"], cast, tile)


def _mixer_inputs(x, mod, w, state, offset):
    shm, scm = [mod[:, i:i + 1] for i in range(3, 5)]
    pos = offset + np.arange(x.shape[1])
    return _proj(x, shm, scm, w["norm_mix"], w["w_in"], pos, w["ret_norm"], state, transposed=state is None)


def _after_attention(x, mod, w, norm_final, o_a, o_r, gate_a, gate_r):
    gm, sh2, sc2, g2 = [mod[:, i:i + 1] for i in range(5, N_MOD)]
    return _merge_ffn(x, gm, o_a, o_r, gate_a, gate_r, w["w_a_proj"], w["w_r_proj"], w["w_o"],
                      sh2, sc2, g2, w["norm_f2"], w["w_up2"], w["w_down2"], norm_final)


def kernel(x_prompt, x_sample, c_prompt, c_sample, cache_k, cache_v, state_ret, norm_f1, w_up1, w_down1, norm_mix, w_in, lambda_q1, lambda_k1, lambda_q2, lambda_k2, da_norm, ret_norm, w_a_proj, w_r_proj, w_o, norm_f2, w_up2, w_down2, w_ada, b_ada, norm_final):
    assert cache_k.shape[0] == 1, "single layer"
    n_prompt = c_prompt.shape[0]
    mod = _modulation(jnp.concatenate([c_prompt, c_sample], axis=0), w_ada[0], b_ada[0])
    mod = mod.reshape(-1, N_MOD, D_MODEL)
    w = {
        "norm_f1": norm_f1, "norm_mix": norm_mix, "norm_f2": norm_f2,
        "w_up1": w_up1[0].astype(BF16), "w_down1": w_down1[0].astype(BF16),
        "lambdas": (lambda_q1, lambda_k1, lambda_q2, lambda_k2),
        "ret_norm": ret_norm[0].reshape(1, RET_V_W),
    }
    nf = norm_final.reshape(1, D_MODEL)
    dec_b, past_len = cache_k.shape[1], cache_k.shape[2]
    cache_kt = jnp.transpose(cache_k[0], (0, 2, 3, 4, 1)).reshape(dec_b, DA_HEADS, HEAD_W, past_len)
    cache_v4 = cache_v[0].reshape(dec_b, past_len * DA_HEADS, HEAD_W)
    mod_p, mod_s = mod[:n_prompt], mod[n_prompt:]
    bp, seq = x_prompt.shape[:2]
    ln = x_sample.shape[1]

    x_s, (w["w_in"],) = _first_half_step(x_sample, mod_s, w, (w_in[0],), TOKEN_TILE)
    q_s, k_s, kb_s, v_s, vb_s, or_s, ga_s, gr_s, st_s = _mixer_inputs(x_s, mod_s, w, state_ret[0], past_len)
    later = (w_a_proj[0], w_r_proj[0], w_o[0], w_up2[0], w_down2[0])
    x_p, (w["w_a_proj"], w["w_r_proj"], w["w_o"], w["w_up2"], w["w_down2"]) = _first_half_step(
        x_prompt, mod_p, w, later, FFN_TOKEN_TILE)
    qt_p, kb_p, kt_p, vt_p, v4_p, or_p, ga_p, gr_p, st_p = _mixer_inputs(x_p, mod_p, w, None, 0)
    oa_p, oa_s = _attention(qt_p, kb_p, vt_p, q_s, kb_s, vb_s, cache_kt, cache_v4, w["lambdas"],
                            da_norm[0].reshape(DA_HEADS, 1, HEAD_W))
    y_p = _after_attention(x_p, mod_p, w, nf, oa_p, or_p, ga_p, gr_p)
    y_s = _after_attention(x_s, mod_s, w, nf, oa_s, or_s, ga_s, gr_s)

    k_new_p = jnp.transpose(kt_p.reshape(bp, DA_HEADS, 2, DA_HD, seq), (0, 4, 1, 2, 3))[None]
    v_new_p = v4_p.reshape(1, bp, seq, DA_HEADS, HEAD_W)
    k_new_s = k_s[None]
    v_new_s = v_s.reshape(1, dec_b, ln, DA_HEADS, HEAD_W)
    return y_p, y_s, k_new_p, v_new_p, st_p[None], k_new_s, v_new_s, st_s[None]
```

```python
import functools
import math

import jax
import jax.numpy as jnp
import numpy as np
from jax import lax
from jax.experimental import pallas as pl
from jax.experimental.pallas import tpu as pltpu

F32 = jnp.float32
BF16 = jnp.bfloat16

D_MODEL = 1024
D_FF = 2816
N_MOD = 9
CHUNK = 64
DA_HEADS = 4
DA_HD = 64
DA_W = 512
HEAD_W = 128
RET_HEADS = 4
RET_QK = 64
RET_QK_W = 256
RET_VD = 128
RET_V_W = 512
ROPE_BASE = 10000.0
EPS = 1e-6
NEG_INF = -1e30
LAM_INIT = 0.8 - 0.6 * math.exp(-0.3 * 0)
LOG2_E = math.log2(math.e)
ONES_ROWS = 16

_IN_SIZES = (DA_W, DA_W, DA_W, RET_QK_W, RET_QK_W, RET_V_W, RET_V_W, D_MODEL, D_MODEL)
_IN_OFF = [0]
for _s in _IN_SIZES:
    _IN_OFF.append(_IN_OFF[-1] + _s)

TOKEN_TILE = 512
FFN_TOKEN_TILE = 1024
BF16_ROW_TILE = 16
ATTN_T = 256
SAMPLE_HEADS_PER_STEP = 2
RET_BLOCK = 256
V7X_VMEM_BYTES = 64 * 1024 * 1024
VMEM_LIMIT = V7X_VMEM_BYTES * 7 // 8


def _const_spec(shape):
    nd = len(shape)
    return pl.BlockSpec(shape, lambda *_: (0,) * nd, pipeline_mode=pl.Buffered(1))


def _params(*sem):
    return pltpu.CompilerParams(dimension_semantics=sem, vmem_limit_bytes=VMEM_LIMIT)


def _rms(x, w):
    ms = jnp.mean(x * x, axis=-1, keepdims=True)
    return x * lax.rsqrt(ms + EPS) * w


def _sigmoid(x):
    return 0.5 * (1.0 + jnp.tanh(0.5 * x))


def _silu(x):
    return x * _sigmoid(x)


def _dot(a, b):
    return jnp.dot(a, b, preferred_element_type=F32)


def _dot_nt(a, b):
    return lax.dot_general(a, b, (((1,), (1,)), ((), ())), preferred_element_type=F32)


def _dot_tn(a, b):
    return lax.dot_general(a, b, (((0,), (0,)), ((), ())), preferred_element_type=F32)


def _mod_kernel(c_ref, w_ref, b_ref, o_ref):
    s = _silu(c_ref[...]).astype(BF16)
    o_ref[...] = _dot(s, w_ref[...].astype(BF16)) + b_ref[...]


def _modulation(c, w_ada, b_ada):
    n, width = c.shape[0], w_ada.shape[1]
    tn = width // 8
    return pl.pallas_call(
        _mod_kernel,
        grid=(width // tn,),
        in_specs=[_const_spec((n, D_MODEL)),
                  pl.BlockSpec((D_MODEL, tn), lambda j: (0, j)),
                  pl.BlockSpec((1, tn), lambda j: (0, j))],
        out_specs=pl.BlockSpec((n, tn), lambda j: (0, j)),
        out_shape=jax.ShapeDtypeStruct((n, width), F32),
        compiler_params=_params("parallel"),
        name="mod",
    )(c, w_ada, b_ada.reshape(1, width))


def _token_grid(x, tile=TOKEN_TILE):
    bsz, seq, _ = x.shape
    ln = min(seq, tile)
    nb = tile // ln
    assert seq % ln == 0 and bsz % nb == 0
    return nb, ln, (bsz // nb, seq // ln)


def _tok_spec(nb, ln, width):
    return pl.BlockSpec((nb, ln, width), lambda b, s: (b, s, 0))


def _seq_spec(nb):
    return pl.BlockSpec((nb, 1, D_MODEL), lambda b, s: (b, 0, 0))


def _half_step(x, sh_ref, sc_ref, g_ref, nw_ref, wup_ref, wdn_ref):
    nb, ln, _ = x.shape
    h = _rms(x, nw_ref[...] * (1.0 + sc_ref[...])) + sh_ref[...]
    hb = h.reshape(nb * ln, D_MODEL).astype(BF16)
    a = _dot(hb, wup_ref[:, :D_FF])
    b = _dot(hb, wup_ref[:, D_FF:])
    act2 = (a * b * (1.0 + jnp.tanh(0.5 * a))).astype(BF16)
    y2 = _dot(act2, wdn_ref[...]).reshape(nb, ln, D_MODEL)
    return x + (0.25 * g_ref[...]) * y2


def _ffn_kernel(*refs, cast_steps, grid_steps):
    n = len(cast_steps)
    x_ref, sh_ref, sc_ref, g_ref, nw_ref, wup_ref, wdn_ref = refs[:7]
    src_refs, o_ref, dst_refs = refs[7:7 + n], refs[7 + n], refs[8 + n:]
    o_ref[...] = _half_step(x_ref[...], sh_ref, sc_ref, g_ref, nw_ref, wup_ref, wdn_ref)
    step = pl.program_id(0) * pl.num_programs(1) + pl.program_id(1)
    for src_ref, dst_ref, steps in zip(src_refs, dst_refs, cast_steps):
        def cast_chunk(src_ref=src_ref, dst_ref=dst_ref):
            dst_ref[...] = src_ref[...].astype(BF16)
        if steps == grid_steps:
            cast_chunk()
        else:
            pl.when(step < steps)(cast_chunk)


def _cast_chunks(rows, n_steps):
    return max(d for d in range(1, n_steps + 1) if rows % d == 0 and (rows // d) % BF16_ROW_TILE == 0)


def _ffn(x, shift, scale, gate, norm_w, w_up, w_down, cast=(), tile=TOKEN_TILE):
    nb, ln, grid = _token_grid(x, tile)
    gs = grid[1]
    cast_steps = tuple(_cast_chunks(w.shape[0], grid[0] * gs) for w in cast)

    def chunk_spec(w, steps):
        return pl.BlockSpec((w.shape[0] // steps, w.shape[1]), lambda b, s: (jnp.minimum(b * gs + s, steps - 1), 0))

    chunk_specs = [chunk_spec(w, steps) for w, steps in zip(cast, cast_steps)]
    outs = pl.pallas_call(
        functools.partial(_ffn_kernel, cast_steps=cast_steps, grid_steps=grid[0] * gs),
        grid=grid,
        in_specs=[_tok_spec(nb, ln, D_MODEL), _seq_spec(nb), _seq_spec(nb), _seq_spec(nb),
                  _const_spec((1, D_MODEL)), _const_spec(w_up.shape), _const_spec(w_down.shape)] + chunk_specs,
        out_specs=[_tok_spec(nb, ln, D_MODEL)] + chunk_specs,
        out_shape=[jax.ShapeDtypeStruct(x.shape, F32)] + [jax.ShapeDtypeStruct(w.shape, BF16) for w in cast],
        compiler_params=_params("arbitrary", "arbitrary"),
        name="ffn",
    )(x, shift, scale, gate, norm_w, w_up, w_down, *cast)
    return outs[0], outs[1:]


def _rotary(x, cos, sin_signed):
    lane = lax.broadcasted_iota(jnp.int32, x.shape, 1)
    first_half = (lane % RET_QK) < (RET_QK // 2)
    width = x.shape[1]
    partner = jnp.where(first_half, pltpu.roll(x, width - RET_QK // 2, 1), pltpu.roll(x, RET_QK // 2, 1))
    return x * cos + partner * sin_signed


def _proj_kernel(*refs, transposed, has_state):
    (x_ref, sh_ref, sc_ref, nw_ref, win_ref, cos_ref, sin_ref,
     dec_ref, qd_ref, kd_ref, sd_ref, rn_ref), refs = refs[:12], refs[12:]
    if has_state:
        s0_ref, refs = refs[0], refs[1:]
    a_refs, (or_ref, ga_ref, gr_ref, sout_ref, st_ref) = refs[:5], refs[5:]
    x = x_ref[...]
    nb, ln, _ = x.shape
    h = _rms(x, nw_ref[...] * (1.0 + sc_ref[...])) + sh_ref[...]
    hb = h.reshape(nb * ln, D_MODEL).astype(BF16)

    def proj(i):
        return _dot(hb, win_ref[:, _IN_OFF[i]:_IN_OFF[i + 1]])

    def put(ref, val):
        ref[...] = val.reshape(nb, ln, val.shape[-1]).astype(ref.dtype)

    q = proj(0) * (DA_HD ** -0.5 * LOG2_E)
    k = proj(1)
    v = proj(2)
    if transposed:
        qt_ref, kb_ref, kt_ref, vt_ref, v4_ref = a_refs
        qt_ref[0] = q.T.astype(BF16)
        kt_ref[0] = k.T
        vt_ref[0] = v.T.astype(BF16)
        for hd in range(DA_HEADS):
            kb_ref[0, hd] = k[:, hd * HEAD_W:(hd + 1) * HEAD_W].astype(BF16)
            v4_ref[0, pl.ds(hd, ln, stride=DA_HEADS), :] = v[:, hd * HEAD_W:(hd + 1) * HEAD_W]
    else:
        q_ref, k_ref, kb_ref, v_ref, vb_ref = a_refs
        put(q_ref, q)
        for n in range(nb):
            for hd in range(DA_HEADS):
                for c in range(2):
                    lo = hd * HEAD_W + c * DA_HD
                    k_ref[n, :, hd, c, :] = k[n * ln:(n + 1) * ln, lo:lo + DA_HD]
        put(kb_ref, k)
        for n in range(nb):
            for hd in range(DA_HEADS):
                v_ref[n, pl.ds(hd, ln, stride=DA_HEADS), :] = v[n * ln:(n + 1) * ln, hd * HEAD_W:(hd + 1) * HEAD_W]
        put(vb_ref, v)
    put(ga_ref, proj(7) * 0.5)
    put(gr_ref, proj(8) * 0.5)

    seq_step = pl.program_id(1)

    @pl.when(seq_step == 0)
    def _():
        st_ref[...] = jnp.zeros(st_ref.shape, F32)
        if has_state:
            for n in range(nb):
                for hd in range(RET_HEADS):
                    st_ref[n, hd * RET_QK:(hd + 1) * RET_QK, hd * RET_VD:(hd + 1) * RET_VD] = s0_ref[n, hd]

    cos, sin_signed = cos_ref[...], sin_ref[...]
    q_r = _rotary(proj(3), cos, sin_signed).astype(BF16)
    k_r = _rotary(proj(4), cos, sin_signed) * (RET_QK ** -0.5)
    v_r = proj(5).astype(BF16)
    z_r = proj(6)
    rl = dec_ref.shape[-1]
    for n in range(nb):
        for blk in range(ln // rl):
            rows = slice(n * ln + blk * rl, n * ln + (blk + 1) * rl)
            o, st_ref[n] = _retention_block(q_r[rows], k_r[rows], v_r[rows], st_ref[n],
                                            dec_ref, qd_ref, kd_ref, sd_ref)
            z = z_r[rows]
            o = o * (0.5 * rn_ref[...]) * (z * (1.0 + jnp.tanh(0.5 * z)))
            or_ref[n, blk * rl:(blk + 1) * rl, :] = o.astype(or_ref.dtype)

    @pl.when(seq_step == pl.num_programs(1) - 1)
    def _():
        for n in range(nb):
            for hd in range(RET_HEADS):
                sout_ref[n, hd] = st_ref[n, hd * RET_QK:(hd + 1) * RET_QK, hd * RET_VD:(hd + 1) * RET_VD]


def _rope_tables(pos, nb):
    half = RET_QK // 2
    inv = ROPE_BASE ** (-np.arange(half, dtype=np.float64) / half)
    ang = np.asarray(pos, np.float64)[:, None] * inv[None, :]
    cos, sin = np.cos(ang), np.sin(ang)
    cos = np.tile(np.concatenate([cos, cos], axis=-1), (nb, RET_HEADS))
    sin = np.tile(np.concatenate([-sin, sin], axis=-1), (nb, RET_HEADS))
    return jnp.asarray(cos, F32), jnp.asarray(sin, F32)


def _proj(x, shift, scale, norm_w, w_in, pos, ret_norm, state, transposed):
    bsz, seq, _ = x.shape
    nb, ln, grid = _token_grid(x)
    cos, sin = _rope_tables(pos, nb)
    tab_spec = pl.BlockSpec((nb * ln, RET_QK_W), lambda b, s: (s, 0))
    rl = min(ln, RET_BLOCK)
    assert ln % rl == 0
    tables = _retention_tables(rl)
    has_state = state is not None
    state_spec = pl.BlockSpec((nb, RET_HEADS, RET_QK, RET_VD), lambda b, s: (b, 0, 0, 0))
    state_shape = jax.ShapeDtypeStruct((bsz, RET_HEADS, RET_QK, RET_VD), F32)
    if transposed:
        assert nb == 1
        t_spec = pl.BlockSpec((1, DA_W, ln), lambda b, s: (b, 0, s))
        t_shape = (bsz, DA_W, seq)
        a_specs = [t_spec, pl.BlockSpec((1, DA_HEADS, ln, HEAD_W), lambda b, s: (b, 0, s, 0)), t_spec, t_spec,
                   pl.BlockSpec((1, DA_HEADS * ln, HEAD_W), lambda b, s: (b, s, 0))]
        a_shapes = [jax.ShapeDtypeStruct(t_shape, BF16), jax.ShapeDtypeStruct((bsz, DA_HEADS, seq, HEAD_W), BF16),
                    jax.ShapeDtypeStruct(t_shape, F32), jax.ShapeDtypeStruct(t_shape, BF16),
                    jax.ShapeDtypeStruct((bsz, DA_HEADS * seq, HEAD_W), F32)]
    else:
        rows = _tok_spec(nb, ln, DA_W)
        a_specs = [rows, pl.BlockSpec((nb, ln, DA_HEADS, 2, DA_HD), lambda b, s: (b, s, 0, 0, 0)), rows,
                   pl.BlockSpec((nb, DA_HEADS * ln, HEAD_W), lambda b, s: (b, s, 0)), rows]
        a_shapes = [jax.ShapeDtypeStruct((bsz, seq, DA_W), BF16),
                    jax.ShapeDtypeStruct((bsz, seq, DA_HEADS, 2, DA_HD), F32),
                    jax.ShapeDtypeStruct((bsz, seq, DA_W), BF16),
                    jax.ShapeDtypeStruct((bsz, DA_HEADS * seq, HEAD_W), F32),
                    jax.ShapeDtypeStruct((bsz, seq, DA_W), BF16)]
    widths = (RET_V_W, D_MODEL, D_MODEL)
    in_specs = [_tok_spec(nb, ln, D_MODEL), _seq_spec(nb), _seq_spec(nb),
                _const_spec((1, D_MODEL)), _const_spec(w_in.shape), tab_spec, tab_spec]
    in_specs += [_const_spec(tab.shape) for tab in tables] + [_const_spec((1, RET_V_W))]
    args = [x, shift, scale, norm_w, w_in, cos, sin, *tables, ret_norm]
    if has_state:
        in_specs.append(state_spec)
        args.append(state)
    return pl.pallas_call(
        functools.partial(_proj_kernel, transposed=transposed, has_state=has_state),
        grid=grid,
        in_specs=in_specs,
        out_specs=a_specs + [_tok_spec(nb, ln, w) for w in widths] + [state_spec],
        out_shape=a_shapes + [jax.ShapeDtypeStruct((bsz, seq, w), BF16) for w in widths] + [state_shape],
        scratch_shapes=[pltpu.VMEM((nb, RET_QK_W, RET_V_W), F32)],
        compiler_params=_params("parallel", "arbitrary"),
        name="proj_t" if transposed else "proj",
    )(*args)


def _lambda(lq1_ref, lk1_ref, lq2_ref, lk2_ref):
    s1 = jnp.sum(lq1_ref[...] * lk1_ref[...], axis=-1, keepdims=True)
    s2 = jnp.sum(lq2_ref[...] * lk2_ref[...], axis=-1, keepdims=True)
    return jnp.exp(s1) - jnp.exp(s2) + LAM_INIT


def _stack_maps(q):
    lane = lax.broadcasted_iota(jnp.int32, q.shape, 1)
    zero = jnp.zeros_like(q)
    return jnp.concatenate([jnp.where(lane < DA_HD, q, zero), jnp.where(lane >= DA_HD, q, zero)], axis=0)


def _prompt_attention(lam, nw_ref, qt_ref, k_ref, vt_ref, o_ref):
    heads, seq = k_ref.shape[1], k_ref.shape[2]
    t = ATTN_T
    feat = lax.broadcasted_iota(jnp.int32, (HEAD_W, t), 0)
    key_chunk = lax.broadcasted_iota(jnp.int32, (t, t), 0) // CHUNK
    query_chunk = lax.broadcasted_iota(jnp.int32, (t, t), 1) // CHUNK
    visible = key_chunk <= query_chunk
    visible2 = jnp.concatenate([visible, visible], axis=1)

    def head_cols(h):
        return slice(h * HEAD_W, (h + 1) * HEAD_W)

    def query_maps(i, h):
        qt = qt_ref[0, head_cols(h), i * t:(i + 1) * t]
        zero = jnp.zeros_like(qt)
        return jnp.concatenate([jnp.where(feat < DA_HD, qt, zero), jnp.where(feat >= DA_HD, qt, zero)], axis=1)

    def scores(i, h):
        lo, hi = i * t, (i + 1) * t
        qm = query_maps(i, h)
        s_diag = jnp.where(visible2, _dot(k_ref[0, h, lo:hi, :], qm), NEG_INF)
        m = jnp.max(s_diag, axis=0, keepdims=True)
        s_full = None
        if i > 0:
            s_full = _dot(k_ref[0, h, :lo, :], qm)
            m = jnp.maximum(m, jnp.max(s_full, axis=0, keepdims=True))
        return s_diag, s_full, m

    ones = jnp.ones((ONES_ROWS, seq), BF16)
    vt1 = [jnp.concatenate([vt_ref[0, head_cols(h), :], ones], axis=0) for h in range(heads)]

    def finish(i, h, s_diag, s_full, m):
        lo, hi = i * t, (i + 1) * t
        acc = _dot(vt1[h][:, lo:hi], jnp.exp2(s_diag - m).astype(BF16))
        if i > 0:
            acc = acc + _dot(vt1[h][:, :lo], jnp.exp2(s_full - m).astype(BF16))
        o2 = acc[:HEAD_W] / acc[HEAD_W:HEAD_W + 1]
        ot = o2[:, :t] - lam * o2[:, t:]
        o_ref[0, h, lo:hi, :] = (_rms(ot.T, nw_ref[h]) * (1.0 - LAM_INIT)).astype(o_ref.dtype)

    n_blocks = seq // t
    pending = [scores(0, h) for h in range(heads)]
    yield
    for i in range(1, n_blocks):
        upcoming = [scores(i, h) for h in range(heads)]
        for h in range(heads):
            finish(i - 1, h, *pending[h])
        pending = upcoming
        yield
    for h in range(heads):
        finish(n_blocks - 1, h, *pending[h])


def _sample_attention(lam, nw_ref, heads, q_ref, kn_ref, vn_ref, kc_ref, vc_ref, o_ref):
    ln = q_ref.shape[1]
    past = kc_ref.shape[-1]
    ones = jnp.ones((past, HEAD_W), BF16)

    def head_cols(j):
        return slice(j * HEAD_W, (j + 1) * HEAD_W)

    def scores(j):
        qs = _stack_maps(q_ref[0, :, head_cols(j)])
        s_cache = _dot(qs, kc_ref[0, j].astype(BF16))
        s_new = _dot_nt(qs, kn_ref[0, :, head_cols(j)])
        m = jnp.maximum(jnp.max(s_cache, axis=-1, keepdims=True), jnp.max(s_new, axis=-1, keepdims=True))
        return s_cache, s_new, m

    def finish(j, s_cache, s_new, m):
        v_cache = vc_ref[0, pl.ds(heads[j], past, stride=DA_HEADS), :].astype(BF16)
        acc = _dot(jnp.exp2(s_cache - m).astype(BF16), jnp.concatenate([v_cache, ones], axis=1))
        v_new = jnp.concatenate([vn_ref[0, :, head_cols(j)], ones[:ln]], axis=1)
        acc = acc + _dot(jnp.exp2(s_new - m).astype(BF16), v_new)
        o2 = acc[:, :HEAD_W] / acc[:, HEAD_W:HEAD_W + 1]
        o = o2[:ln] - lam * o2[ln:]
        o_ref[0, j] = (_rms(o, nw_ref[j]) * (1.0 - LAM_INIT)).astype(o_ref.dtype)

    pending = scores(0)
    yield
    for j in range(1, len(heads)):
        upcoming = scores(j)
        yield
        finish(j - 1, *pending)
        yield
        pending = upcoming
    finish(len(heads) - 1, *pending)


def _weave(main, side, main_per_side=2):
    done = object()
    main_live = side_live = True
    while main_live or side_live:
        if side_live:
            side_live = next(side, done) is not done
        for _ in range(main_per_side):
            if main_live:
                main_live = next(main, done) is not done


def _attn_kernel(lq1_ref, lk1_ref, lq2_ref, lk2_ref, nwp_ref, nws_ref, qt_ref, k_ref, vt_ref,
                 q_ref, kn_ref, vn_ref, kc_ref, vc_ref, op_ref, os_ref):
    lam = _lambda(lq1_ref, lk1_ref, lq2_ref, lk2_ref)
    g = SAMPLE_HEADS_PER_STEP
    first_head = (pl.program_id(0) % (DA_HEADS // g)) * g
    heads = tuple(first_head + j for j in range(g))
    _weave(_prompt_attention(lam, nwp_ref, qt_ref, k_ref, vt_ref, op_ref),
           _sample_attention(lam, nws_ref, heads, q_ref, kn_ref, vn_ref, kc_ref, vc_ref, os_ref))


def _attention(qt, k, vt, q, k_new, v_new, cache_kt, cache_v4, lambdas, da_norm):
    bp, _, seq, _ = k.shape
    bs, ln, _ = q.shape
    past = cache_kt.shape[-1]
    g = SAMPLE_HEADS_PER_STEP
    groups = DA_HEADS // g
    steps = bp * DA_HEADS
    assert steps == bs * groups, "one prompt (sequence, head) per sample (sequence, head group)"
    assert seq % ATTN_T == 0 and ATTN_T % CHUNK == 0 and past % CHUNK == 0 and ln <= CHUNK
    lam_spec = _const_spec((1, DA_HD))
    t_spec = pl.BlockSpec((1, HEAD_W, seq), lambda i: (i // DA_HEADS, i % DA_HEADS, 0))
    head_spec = pl.BlockSpec((1, 1, seq, HEAD_W), lambda i: (i // DA_HEADS, i % DA_HEADS, 0, 0))
    new_spec = pl.BlockSpec((1, ln, g * HEAD_W), lambda i: (i // groups, 0, i % groups))
    out_new_spec = pl.BlockSpec((1, g, ln, HEAD_W), lambda i: (i // groups, i % groups, 0, 0))
    return pl.pallas_call(
        _attn_kernel,
        grid=(steps,),
        in_specs=[lam_spec] * 4 + [pl.BlockSpec((1, 1, HEAD_W), lambda i: (i % DA_HEADS, 0, 0)),
                                   pl.BlockSpec((g, 1, HEAD_W), lambda i: (i % groups, 0, 0)), t_spec, head_spec, t_spec,
                                   new_spec, new_spec, new_spec,
                                   pl.BlockSpec((1, g, HEAD_W, past), lambda i: (i // groups, i % groups, 0, 0)),
                                   pl.BlockSpec((1, DA_HEADS * past, HEAD_W), lambda i: (i // groups, 0, 0))],
        out_specs=[head_spec, out_new_spec],
        out_shape=[jax.ShapeDtypeStruct((bp, DA_HEADS, seq, HEAD_W), BF16),
                   jax.ShapeDtypeStruct((bs, DA_HEADS, ln, HEAD_W), BF16)],
        compiler_params=_params("parallel"),
        name="attn",
    )(*lambdas, da_norm, da_norm, qt, k, vt, q, k_new, v_new, cache_kt, cache_v4)


def _retention_block(q, k_f32, v, state, dec_ref, qd_ref, kd_ref, sd_ref):
    k = k_f32.astype(BF16)
    cross = _dot(q, state.astype(BF16)) * qd_ref[...]
    lane = lax.broadcasted_iota(jnp.int32, q.shape, 1)
    outs = []
    for h in range(RET_HEADS):
        qh = jnp.where((lane >= h * RET_QK) & (lane < (h + 1) * RET_QK), q, jnp.zeros_like(q))
        scores = _dot_nt(qh, k) * dec_ref[h]
        cols = slice(h * RET_VD, (h + 1) * RET_VD)
        oh = _dot(scores.astype(BF16), v[:, cols]) + cross[:, cols]
        xc = oh - jnp.mean(oh, axis=-1, keepdims=True)
        outs.append(xc * lax.rsqrt(jnp.mean(xc * xc, axis=-1, keepdims=True) + EPS))
    kd = (k_f32 * kd_ref[...]).astype(BF16)
    upd = _dot_tn(kd, v)
    r = lax.broadcasted_iota(jnp.int32, upd.shape, 0) // RET_QK
    cl = lax.broadcasted_iota(jnp.int32, upd.shape, 1) // RET_VD
    new_state = state * sd_ref[...] + jnp.where(r == cl, upd, 0.0)
    return jnp.concatenate(outs, axis=-1), new_state


def _retention_tables(ln):
    log_g = np.log(1.0 - 2.0 ** (-5.0 - np.arange(RET_HEADS, dtype=np.float64)))
    idx = np.arange(ln, dtype=np.float64)
    dist = idx[:, None] - idx[None, :]
    decay = np.where(dist >= 0, np.exp(np.maximum(dist, 0.0)[None] * log_g[:, None, None]), 0.0)
    q_decay = np.exp((idx + 1.0)[:, None] * log_g[None, :])
    k_decay = np.exp((ln - 1.0 - idx)[:, None] * log_g[None, :])
    s_decay = np.exp(ln * log_g)[None, :]
    tables = (decay, np.repeat(q_decay, RET_VD, axis=1), np.repeat(k_decay, RET_QK, axis=1),
              np.repeat(s_decay, RET_VD, axis=1))
    return tuple(jnp.asarray(tab, F32) for tab in tables)


def _merge_ffn_kernel(x_ref, gm_ref, oa_ref, or_ref, ga_ref, gr_ref, wa_ref, wr_ref, wo_ref,
                      sh_ref, sc_ref, g_ref, nw_ref, wup_ref, wdn_ref, nf_ref, o_ref):
    x = x_ref[...]
    nb, ln, _ = x.shape

    def flat(ref):
        return ref[...].reshape(nb * ln, ref.shape[-1])

    o_a = jnp.concatenate([oa_ref[:, hd] for hd in range(DA_HEADS)], axis=-1)
    a = _dot(o_a.reshape(nb * ln, DA_W), wa_ref[...])
    r = _dot(flat(or_ref), wr_ref[...])
    merged2 = (1.0 + jnp.tanh(flat(ga_ref).astype(F32))) * a + (1.0 + jnp.tanh(flat(gr_ref).astype(F32))) * r
    y2 = _dot(merged2.astype(BF16), wo_ref[...]).reshape(nb, ln, D_MODEL)
    x = x + (0.5 * gm_ref[...]) * y2
    x = _half_step(x, sh_ref, sc_ref, g_ref, nw_ref, wup_ref, wdn_ref)
    o_ref[...] = _rms(x, nf_ref[...])


def _merge_ffn(x, gate_m, o_a, o_r, gate_a, gate_r, w_a, w_r, w_o, shift, scale, gate, norm_w, w_up, w_down,
               norm_final):
    nb, ln, grid = _token_grid(x)
    return pl.pallas_call(
        _merge_ffn_kernel,
        grid=grid,
        in_specs=[_tok_spec(nb, ln, D_MODEL), _seq_spec(nb),
                  pl.BlockSpec((nb, DA_HEADS, ln, HEAD_W), lambda b, s: (b, 0, s, 0)), _tok_spec(nb, ln, RET_V_W),
                  _tok_spec(nb, ln, D_MODEL), _tok_spec(nb, ln, D_MODEL),
                  _const_spec(w_a.shape), _const_spec(w_r.shape), _const_spec(w_o.shape),
                  _seq_spec(nb), _seq_spec(nb), _seq_spec(nb),
                  _const_spec((1, D_MODEL)), _const_spec(w_up.shape), _const_spec(w_down.shape),
                  _const_spec((1, D_MODEL))],
        out_specs=_tok_spec(nb, ln, D_MODEL),
        out_shape=jax.ShapeDtypeStruct(x.shape, F32),
        compiler_params=_params("parallel", "parallel"),
        name="merge_ffn",
    )(x, gate_m, o_a, o_r, gate_a, gate_r, w_a, w_r, w_o, shift, scale, gate, norm_w, w_up, w_down, norm_final)


def _first_half_step(x, mod, w, cast, tile):
    sh1, sc1, g1 = [mod[:, i:i + 1] for i in range(3)]
    return _ffn(x, sh1, sc1, g1, w["norm_f1"], w["w_up1"], w["w_down1"], cast, tile)


def _mixer_inputs(x, mod, w, state, offset):
    shm, scm = [mod[:, i:i + 1] for i in range(3, 5)]
    pos = offset + np.arange(x.shape[1])
    return _proj(x, shm, scm, w["norm_mix"], w["w_in"], pos, w["ret_norm"], state, transposed=state is None)


def _after_attention(x, mod, w, norm_final, o_a, o_r, gate_a, gate_r):
    gm, sh2, sc2, g2 = [mod[:, i:i + 1] for i in range(5, N_MOD)]
    return _merge_ffn(x, gm, o_a, o_r, gate_a, gate_r, w["w_a_proj"], w["w_r_proj"], w["w_o"],
                      sh2, sc2, g2, w["norm_f2"], w["w_up2"], w["w_down2"], norm_final)


def kernel(x_prompt, x_sample, c_prompt, c_sample, cache_k, cache_v, state_ret, norm_f1, w_up1, w_down1, norm_mix, w_in, lambda_q1, lambda_k1, lambda_q2, lambda_k2, da_norm, ret_norm, w_a_proj, w_r_proj, w_o, norm_f2, w_up2, w_down2, w_ada, b_ada, norm_final):
    assert cache_k.shape[0] == 1, "single layer"
    n_prompt = c_prompt.shape[0]
    mod = _modulation(jnp.concatenate([c_prompt, c_sample], axis=0), w_ada[0], b_ada[0])
    mod = mod.reshape(-1, N_MOD, D_MODEL)
    w = {
        "norm_f1": norm_f1, "norm_mix": norm_mix, "norm_f2": norm_f2,
        "w_up1": w_up1[0].astype(BF16), "w_down1": w_down1[0].astype(BF16),
        "lambdas": (lambda_q1, lambda_k1, lambda_q2, lambda_k2),
        "ret_norm": ret_norm[0].reshape(1, RET_V_W),
    }
    nf = norm_final.reshape(1, D_MODEL)
    dec_b, past_len = cache_k.shape[1], cache_k.shape[2]
    cache_kt = jnp.transpose(cache_k[0], (0, 2, 3, 4, 1)).reshape(dec_b, DA_HEADS, HEAD_W, past_len)
    cache_v4 = cache_v[0].reshape(dec_b, past_len * DA_HEADS, HEAD_W)
    mod_p, mod_s = mod[:n_prompt], mod[n_prompt:]
    bp, seq = x_prompt.shape[:2]
    ln = x_sample.shape[1]

    x_s, (w["w_in"],) = _first_half_step(x_sample, mod_s, w, (w_in[0],), TOKEN_TILE)
    q_s, k_s, kb_s, v_s, vb_s, or_s, ga_s, gr_s, st_s = _mixer_inputs(x_s, mod_s, w, state_ret[0], past_len)
    later = (w_a_proj[0], w_r_proj[0], w_o[0], w_up2[0], w_down2[0])
    x_p, (w["w_a_proj"], w["w_r_proj"], w["w_o"], w["w_up2"], w["w_down2"]) = _first_half_step(
        x_prompt, mod_p, w, later, FFN_TOKEN_TILE)
    qt_p, kb_p, kt_p, vt_p, v4_p, or_p, ga_p, gr_p, st_p = _mixer_inputs(x_p, mod_p, w, None, 0)
    oa_p, oa_s = _attention(qt_p, kb_p, vt_p, q_s, kb_s, vb_s, cache_kt, cache_v4, w["lambdas"],
                            da_norm[0].reshape(DA_HEADS, 1, HEAD_W))
    y_p = _after_attention(x_p, mod_p, w, nf, oa_p, or_p, ga_p, gr_p)
    y_s = _after_attention(x_s, mod_s, w, nf, oa_s, or_s, ga_s, gr_s)

    k_new_p = jnp.transpose(kt_p.reshape(bp, DA_HEADS, 2, DA_HD, seq), (0, 4, 1, 2, 3))[None]
    v_new_p = v4_p.reshape(1, bp, seq, DA_HEADS, HEAD_W)
    k_new_s = k_s[None]
    v_new_s = v_s.reshape(1, dec_b, ln, DA_HEADS, HEAD_W)
    return y_p, y_s, k_new_p, v_new_p, st_p[None], k_new_s, v_new_s, st_s[None]
```
